```python
import math
import jax, jax.numpy as jnp
from jax import lax
import numpy as np

D_MODEL = 2048
BATCH = 4
SEQ = 4096
DEPTH = 2

D_FOURIER = D_MODEL // 2
FOURIER_GROUPS = 8
FOURIER_GROUP_DIM = D_FOURIER // FOURIER_GROUPS
V_HEAD_DIM = 128
D_ATTN = D_MODEL // 2
N_HEADS = D_ATTN // V_HEAD_DIM
QK_NOPE_DIM = 128
QK_ROPE_DIM = 64
QK_HEAD_DIM = QK_NOPE_DIM + QK_ROPE_DIM
Q_LORA_RANK = D_MODEL // 4
KV_LORA_RANK = D_MODEL // 8
D_MIX = D_FOURIER + D_ATTN
IN_SPLITS = (D_FOURIER, D_FOURIER, Q_LORA_RANK, KV_LORA_RANK, QK_ROPE_DIM, D_ATTN)
D_IN = sum(IN_SPLITS)
ROPE_THETA = 10000.0
Q_BLOCK = 128
NORM_EPS = 1e-6
DEEPNORM_ALPHA = (2 * DEPTH) ** 0.25
DEEPNORM_BETA = (8 * DEPTH) ** -0.25

kernel_name = "hybrid_fnet_mla_deepnorm_adaln"


def _layernorm(x, g=None, b=None):
    xf = x.astype(jnp.float32)
    mu = jnp.mean(xf, axis=-1, keepdims=True)
    var = jnp.mean(jnp.square(xf - mu), axis=-1, keepdims=True)
    y = (xf - mu) * lax.rsqrt(var + NORM_EPS)
    if g is not None:
        y = y * g.astype(jnp.float32) + b.astype(jnp.float32)
    return y.astype(x.dtype)


def _rmsnorm(x, g):
    xf = x.astype(jnp.float32)
    y = xf * lax.rsqrt(jnp.mean(jnp.square(xf), axis=-1, keepdims=True) + NORM_EPS)
    return (y * g.astype(jnp.float32)).astype(x.dtype)


def _rope_tables(positions, dtype):
    inv_freq = ROPE_THETA ** (-jnp.arange(0, QK_ROPE_DIM, 2, dtype=jnp.float32) / QK_ROPE_DIM)
    ang = positions.astype(jnp.float32)[..., None] * inv_freq
    return jnp.cos(ang).astype(dtype), jnp.sin(ang).astype(dtype)


def _apply_rope(x, cos, sin):
    x1, x2 = jnp.split(x, 2, axis=-1)
    return jnp.concatenate([x1 * cos - x2 * sin, x2 * cos + x1 * sin], axis=-1)


def _fourier_mix(u, w_fmix):
    B, S, _ = u.shape
    ug = u.reshape(B, S, FOURIER_GROUPS, FOURIER_GROUP_DIM).astype(jnp.float32)
    f = jnp.real(jnp.fft.fftn(ug, axes=(1, 3), norm="ortho")).astype(u.dtype)
    y = jnp.einsum("bsgc,gcd->bsgd", f, w_fmix)
    return y.reshape(B, S, D_FOURIER)


def _mla_attention(q_nope, q_rope, k_nope, k_rope, v):
    B, S, H, _ = q_nope.shape
    nb = S // Q_BLOCK
    scale = 1.0 / math.sqrt(QK_HEAD_DIM)
    qn = q_nope.reshape(B, nb, Q_BLOCK, H, QK_NOPE_DIM).transpose(1, 0, 2, 3, 4)
    qr = q_rope.reshape(B, nb, Q_BLOCK, H, QK_ROPE_DIM).transpose(1, 0, 2, 3, 4)

    def block(args):
        qn_b, qr_b = args
        s = (jnp.einsum("bqhd,bkhd->bhqk", qn_b, k_nope)
             + jnp.einsum("bqhr,bkr->bhqk", qr_b, k_rope)).astype(jnp.float32) * scale
        p = jax.nn.softmax(s, axis=-1).astype(v.dtype)
        return jnp.einsum("bhqk,bkhd->bqhd", p, v)

    o = lax.map(block, (qn, qr))
    return o.transpose(1, 0, 2, 3, 4).reshape(B, S, H * V_HEAD_DIM)


def setup_inputs(seed: int = 0) -> dict:
    key = jax.random.key(seed)
    ks = jax.random.split(key, 16)
    f32 = jnp.float32
    x = jax.random.normal(ks[0], (BATCH, SEQ, D_MODEL), f32)
    c = jax.random.normal(ks[1], (BATCH, D_MODEL), f32)
    offs = jax.random.randint(ks[2], (BATCH, 1), 0, SEQ, dtype=jnp.int32)
    positions = (jnp.arange(SEQ, dtype=jnp.int32)[None, :] + offs).astype(jnp.int32)
    w_ada = jax.random.normal(ks[3], (DEPTH, D_MODEL, 3 * D_MODEL), f32) * D_MODEL ** -0.5
    b_ada = 0.02 * jax.random.normal(ks[4], (DEPTH, 3 * D_MODEL), f32)
    w_in = jax.random.normal(ks[5], (DEPTH, D_MODEL, D_IN), f32) * D_MODEL ** -0.5
    q_norm = 1.0 + 0.02 * jax.random.normal(ks[6], (DEPTH, Q_LORA_RANK), f32)
    w_q_b = jax.random.normal(ks[7], (DEPTH, Q_LORA_RANK, N_HEADS * QK_HEAD_DIM), f32) * Q_LORA_RANK ** -0.5
    kv_norm = 1.0 + 0.02 * jax.random.normal(ks[8], (DEPTH, KV_LORA_RANK), f32)
    w_kv_b = jax.random.normal(ks[9], (DEPTH, KV_LORA_RANK, N_HEADS * (QK_NOPE_DIM + V_HEAD_DIM)), f32) * KV_LORA_RANK ** -0.5
    w_fmix = jax.random.normal(ks[10], (DEPTH, FOURIER_GROUPS, FOURIER_GROUP_DIM, FOURIER_GROUP_DIM), f32) * FOURIER_GROUP_DIM ** -0.5
    w_out = jax.random.normal(ks[11], (DEPTH, D_MIX, D_MODEL), f32) * (D_MIX ** -0.5 * DEEPNORM_BETA)
    ln_g = 1.0 + 0.02 * jax.random.normal(ks[12], (DEPTH, D_MODEL), f32)
    ln_b = 0.02 * jax.random.normal(ks[13], (DEPTH, D_MODEL), f32)
    return {"x": x, "c": c, "positions": positions, "w_ada": w_ada, "b_ada": b_ada,
            "w_in": w_in, "q_norm": q_norm, "w_q_b": w_q_b, "kv_norm": kv_norm,
            "w_kv_b": w_kv_b, "w_fmix": w_fmix, "w_out": w_out, "ln_g": ln_g, "ln_b": ln_b}


def reference(x, c, positions, w_ada, b_ada, w_in, q_norm, w_q_b, kv_norm, w_kv_b,
              w_fmix, w_out, ln_g, ln_b):
    B, S, D = x.shape
    cos, sin = _rope_tables(positions, x.dtype)
    cos_q, sin_q = cos[:, :, None, :], sin[:, :, None, :]
    c_act = jax.nn.silu(c)
    split_pts = list(np.cumsum(IN_SPLITS)[:-1])
    for l in range(DEPTH):
        mod = c_act @ w_ada[l] + b_ada[l]
        shift, scale, gate = jnp.split(mod, 3, axis=-1)
        h = _layernorm(x) * (1.0 + scale[:, None, :]) + shift[:, None, :]
        p = h @ w_in[l]
        u_f, z_f, cq, ckv, k_r, z_a = jnp.split(p, split_pts, axis=-1)
        y_f = _fourier_mix(u_f, w_fmix[l]) * jax.nn.silu(z_f)
        q = (_rmsnorm(cq, q_norm[l]) @ w_q_b[l]).reshape(B, S, N_HEADS, QK_HEAD_DIM)
        q_nope, q_rope = q[..., :QK_NOPE_DIM], _apply_rope(q[..., QK_NOPE_DIM:], cos_q, sin_q)
        kv = (_rmsnorm(ckv, kv_norm[l]) @ w_kv_b[l]).reshape(B, S, N_HEADS, QK_NOPE_DIM + V_HEAD_DIM)
        k_nope, v = kv[..., :QK_NOPE_DIM], kv[..., QK_NOPE_DIM:]
        k_rope = _apply_rope(k_r, cos, sin)
        y_a = _mla_attention(q_nope, q_rope, k_nope, k_rope, v) * jax.nn.silu(z_a)
        y = jnp.concatenate([y_f, y_a], axis=-1) @ w_out[l]
        x = _layernorm(DEEPNORM_ALPHA * x + gate[:, None, :] * y, ln_g[l], ln_b[l])
    return x
```

```python
import functools
import math

import numpy as np
import jax
import jax.numpy as jnp
from jax import lax
from jax.experimental import pallas as pl
from jax.experimental.pallas import tpu as pltpu

D_MODEL = 2048
BATCH = 4
SEQ = 4096
DEPTH = 2
D_FOURIER = D_MODEL // 2
FOURIER_GROUPS = 8
FOURIER_GROUP_DIM = D_FOURIER // FOURIER_GROUPS
V_HEAD_DIM = 128
D_ATTN = D_MODEL // 2
N_HEADS = D_ATTN // V_HEAD_DIM
QK_NOPE_DIM = 128
QK_ROPE_DIM = 64
QK_HEAD_DIM = QK_NOPE_DIM + QK_ROPE_DIM
Q_LORA_RANK = D_MODEL // 4
KV_LORA_RANK = D_MODEL // 8
ROPE_THETA = 10000.0
NORM_EPS = 1e-6
DEEPNORM_ALPHA = (2 * DEPTH) ** 0.25

TOKENS = BATCH * SEQ
HALF_ROPE = QK_ROPE_DIM // 2
QK_PAD_DIM = 256
W_IN_COLS = 3968
COL_U, COL_ZF, COL_ZA, COL_CQ, COL_CKV = 0, 1024, 2048, 3072, 3584

F32 = jnp.float32
BF16 = jnp.bfloat16
VMEM_LIMIT = 52 * 1024 * 1024

Q_SCALE = math.log2(math.e) / math.sqrt(QK_HEAD_DIM)


def _silu(v):
    return v * (1.0 / (1.0 + jnp.exp(-v)))


def _nt_dot(a, b):
    return lax.dot_general(a, b, (((1,), (1,)), ((), ())), preferred_element_type=F32)


def _ada_kernel(c_ref, w_ref, b_ref, o_ref):
    c_act = _silu(c_ref[...]).astype(BF16)
    w = w_ref[0].astype(BF16)
    o_ref[0] = jnp.dot(c_act, w, preferred_element_type=F32) + b_ref[0]


def _ada_modulation(c, w_ada, b_ada):
    tn = 768
    c_pad = jnp.pad(c, ((0, 8 - BATCH), (0, 0)))
    b3 = b_ada.reshape(DEPTH, 1, 3 * D_MODEL)
    return pl.pallas_call(
        _ada_kernel,
        out_shape=jax.ShapeDtypeStruct((DEPTH, 8, 3 * D_MODEL), F32),
        grid=(DEPTH, 3 * D_MODEL // tn),
        in_specs=[
            pl.BlockSpec((8, D_MODEL), lambda l, j: (0, 0)),
            pl.BlockSpec((1, D_MODEL, tn), lambda l, j: (l, 0, j)),
            pl.BlockSpec((1, 1, tn), lambda l, j: (l, 0, j)),
        ],
        out_specs=pl.BlockSpec((1, 8, tn), lambda l, j: (l, 0, j)),
        compiler_params=pltpu.CompilerParams(
            dimension_semantics=("arbitrary", "arbitrary"), vmem_limit_bytes=VMEM_LIMIT),
        name="ada_modulation",
    )(c_pad, w_ada, b3)


def _rope_kernel(pos_row_ref, pos_col_ref, invf_col_ref, invf_row_ref,
                 cos_t_ref, sin_t_ref, cos2_ref, sin2_ref):
    ang_t = invf_col_ref[...] * pos_row_ref[0].astype(F32)
    cos_t_ref[0] = jnp.cos(ang_t)
    sin_t_ref[0] = jnp.sin(ang_t)
    ang2 = pos_col_ref[0].astype(F32) * invf_row_ref[...]
    lane = lax.broadcasted_iota(jnp.int32, ang2.shape, 1)
    cos2_ref[0] = jnp.cos(ang2)
    sin2_ref[0] = jnp.where(lane < HALF_ROPE, -jnp.sin(ang2), jnp.sin(ang2))


def _rope_tables(positions):
    ts = 1024
    inv_freq = ROPE_THETA ** (-jnp.arange(0, QK_ROPE_DIM, 2, dtype=F32) / QK_ROPE_DIM)
    invf_col = inv_freq.reshape(HALF_ROPE, 1)
    invf_row = jnp.concatenate([inv_freq, inv_freq]).reshape(1, QK_ROPE_DIM)
    pos_row = positions.reshape(BATCH, 1, SEQ)
    pos_col = positions.reshape(BATCH, SEQ, 1)
    return pl.pallas_call(
        _rope_kernel,
        out_shape=(
            jax.ShapeDtypeStruct((BATCH, HALF_ROPE, SEQ), F32),
            jax.ShapeDtypeStruct((BATCH, HALF_ROPE, SEQ), F32),
            jax.ShapeDtypeStruct((BATCH, SEQ, QK_ROPE_DIM), F32),
            jax.ShapeDtypeStruct((BATCH, SEQ, QK_ROPE_DIM), F32),
        ),
        grid=(BATCH, SEQ // ts),
        in_specs=[
            pl.BlockSpec((1, 1, ts), lambda b, s: (b, 0, s)),
            pl.BlockSpec((1, ts, 1), lambda b, s: (b, s, 0)),
            pl.BlockSpec((HALF_ROPE, 1), lambda b, s: (0, 0)),
            pl.BlockSpec((1, QK_ROPE_DIM), lambda b, s: (0, 0)),
        ],
        out_specs=(
            pl.BlockSpec((1, HALF_ROPE, ts), lambda b, s: (b, 0, s)),
            pl.BlockSpec((1, HALF_ROPE, ts), lambda b, s: (b, 0, s)),
            pl.BlockSpec((1, ts, QK_ROPE_DIM), lambda b, s: (b, s, 0)),
            pl.BlockSpec((1, ts, QK_ROPE_DIM), lambda b, s: (b, s, 0)),
        ),
        compiler_params=pltpu.CompilerParams(
            dimension_semantics=("arbitrary", "arbitrary"), vmem_limit_bytes=VMEM_LIMIT),
        name="rope_tables",
    )(pos_row, pos_col, invf_col, invf_row)


def _channel_dft_matrix():
    idx = np.arange(FOURIER_GROUP_DIM)
    ang = 2.0 * np.pi * ((idx[:, None] * idx[None, :]) % FOURIER_GROUP_DIM) / FOURIER_GROUP_DIM
    scale = 1.0 / math.sqrt(FOURIER_GROUP_DIM)
    return np.concatenate([np.cos(ang), np.sin(ang)], axis=1).astype(np.float32) * np.float32(scale)


def _fold_kernel(cs_ref, w_ref, o_ref):
    w = w_ref[0]
    dims = (((1,), (0,)), ((), ()))
    a = lax.dot_general(cs_ref[:, :FOURIER_GROUP_DIM], w, dims,
                        precision=lax.Precision.HIGHEST, preferred_element_type=F32)
    b = lax.dot_general(cs_ref[:, FOURIER_GROUP_DIM:], w, dims,
                        precision=lax.Precision.HIGHEST, preferred_element_type=F32)
    o_ref[0] = jnp.concatenate([a, b], axis=1).astype(BF16)


def _fold_fourier_weights(w_fmix_l):
    g, c = FOURIER_GROUPS, FOURIER_GROUP_DIM
    return pl.pallas_call(
        _fold_kernel,
        out_shape=jax.ShapeDtypeStruct((g, c, 2 * c), BF16),
        grid=(g,),
        in_specs=[
            pl.BlockSpec((c, 2 * c), lambda i: (0, 0)),
            pl.BlockSpec((1, c, c), lambda i: (i, 0, 0)),
        ],
        out_specs=pl.BlockSpec((1, c, 2 * c), lambda i: (i, 0, 0)),
        compiler_params=pltpu.CompilerParams(dimension_semantics=("arbitrary",)),
        name="fold_fourier_weights",
    )(jnp.asarray(_channel_dft_matrix()), w_fmix_l)


def _token_kernel(x_ref, scale_ref, shift_ref, cos_t_ref, sin_t_ref, cos2_ref, sin2_ref,
                  w_in_ref, qn_ref, kvn_ref, wq_t_ref, wk_ref, wv_t_ref, ab_ref,
                  pq_ref, gf_ref, ga_ref, q_t_ref, k_ref, v_t_ref):
    x = x_ref[...]
    tm = x.shape[0]
    mu = jnp.mean(x, axis=-1, keepdims=True)
    xc = x - mu
    var = jnp.mean(xc * xc, axis=-1, keepdims=True)
    h = xc * lax.rsqrt(var + NORM_EPS) * (1.0 + scale_ref[0]) + shift_ref[0]
    hb = h.astype(BF16)

    def proj(lo, hi):
        return jnp.dot(hb, w_in_ref[:, lo:hi], preferred_element_type=F32)

    ub = proj(COL_U, COL_ZF).astype(BF16)
    c = FOURIER_GROUP_DIM
    for g in range(FOURIER_GROUPS):
        pq = jnp.dot(ub[:, g * c:(g + 1) * c], ab_ref[g], preferred_element_type=F32)
        pq_ref[0, 0, :, g * c:(g + 1) * c] = pq[:, :c].astype(BF16)
        pq_ref[0, 1, :, g * c:(g + 1) * c] = pq[:, c:].astype(BF16)

    gf_ref[...] = _silu(proj(COL_ZF, COL_ZA)).astype(BF16)
    ga_ref[...] = _silu(proj(COL_ZA, COL_CQ)).astype(BF16)

    cq = proj(COL_CQ, COL_CKV)
    cqn = (cq * lax.rsqrt(jnp.mean(cq * cq, axis=-1, keepdims=True) + NORM_EPS)
           * qn_ref[...]).astype(BF16)
    q_t = _nt_dot(wq_t_ref[...], cqn) * Q_SCALE
    cos_t = cos_t_ref[0]
    sin_t = sin_t_ref[0]
    n_nope = N_HEADS * QK_NOPE_DIM
    n_half = N_HEADS * HALF_ROPE
    zeros_q = jnp.zeros((QK_PAD_DIM - QK_HEAD_DIM, tm), BF16)
    for hd in range(N_HEADS):
        q_t_ref[0, hd, 0:QK_NOPE_DIM, :] = (
            q_t[hd * QK_NOPE_DIM:(hd + 1) * QK_NOPE_DIM].astype(BF16))
        x1 = q_t[n_nope + hd * HALF_ROPE:n_nope + (hd + 1) * HALF_ROPE]
        x2 = q_t[n_nope + n_half + hd * HALF_ROPE:n_nope + n_half + (hd + 1) * HALF_ROPE]
        q_t_ref[0, hd, QK_NOPE_DIM:QK_NOPE_DIM + HALF_ROPE, :] = (
            x1 * cos_t - x2 * sin_t).astype(BF16)
        q_t_ref[0, hd, QK_NOPE_DIM + HALF_ROPE:QK_HEAD_DIM, :] = (
            x2 * cos_t + x1 * sin_t).astype(BF16)
        q_t_ref[0, hd, QK_HEAD_DIM:QK_PAD_DIM, :] = zeros_q

    ckv_kr = proj(COL_CKV, W_IN_COLS)
    ckv = ckv_kr[:, :KV_LORA_RANK]
    ckvn = (ckv * lax.rsqrt(jnp.mean(ckv * ckv, axis=-1, keepdims=True) + NORM_EPS)
            * kvn_ref[...]).astype(BF16)
    k_nope = jnp.dot(ckvn, wk_ref[...], preferred_element_type=F32)
    v_t = _nt_dot(wv_t_ref[...], ckvn)
    kr = ckv_kr[:, KV_LORA_RANK:KV_LORA_RANK + QK_ROPE_DIM]
    kr_swapped = jnp.concatenate([kr[:, HALF_ROPE:], kr[:, :HALF_ROPE]], axis=1)
    kr_rot = (kr * cos2_ref[0] + kr_swapped * sin2_ref[0]).astype(BF16)
    kr_pad = jnp.concatenate(
        [kr_rot, jnp.zeros((tm, QK_PAD_DIM - QK_HEAD_DIM), BF16)], axis=1)
    for hd in range(N_HEADS):
        k_ref[0, hd, :, 0:QK_NOPE_DIM] = (
            k_nope[:, hd * QK_NOPE_DIM:(hd + 1) * QK_NOPE_DIM].astype(BF16))
        k_ref[0, hd, :, QK_NOPE_DIM:QK_PAD_DIM] = kr_pad
        v_t_ref[0, hd] = v_t[hd * V_HEAD_DIM:(hd + 1) * V_HEAD_DIM].astype(BF16)


def _resident(shape):
    zeros = (0,) * len(shape)
    return pl.BlockSpec(shape, lambda *_: zeros, pipeline_mode=pl.Buffered(1))


def _token_stage(x2d, scale, shift, tables, w):
    tm = 256
    nt = SEQ // tm
    cos_t, sin_t, cos2, sin2 = tables
    row = lambda i: (i, 0)
    per_batch = lambda i: (i // nt, 0, 0)
    bh = lambda i: (i // nt, 0, 0, i % nt)
    return pl.pallas_call(
        _token_kernel,
        out_shape=(
            jax.ShapeDtypeStruct((BATCH, 2, SEQ, D_FOURIER), BF16),
            jax.ShapeDtypeStruct((TOKENS, D_FOURIER), BF16),
            jax.ShapeDtypeStruct((TOKENS, D_ATTN), BF16),
            jax.ShapeDtypeStruct((BATCH, N_HEADS, QK_PAD_DIM, SEQ), BF16),
            jax.ShapeDtypeStruct((BATCH, N_HEADS, SEQ, QK_PAD_DIM), BF16),
            jax.ShapeDtypeStruct((BATCH, N_HEADS, V_HEAD_DIM, SEQ), BF16),
        ),
        grid=(TOKENS // tm,),
        in_specs=[
            pl.BlockSpec((tm, D_MODEL), row),
            pl.BlockSpec((1, 1, D_MODEL), per_batch),
            pl.BlockSpec((1, 1, D_MODEL), per_batch),
            pl.BlockSpec((1, HALF_ROPE, tm), lambda i: (i // nt, 0, i % nt)),
            pl.BlockSpec((1, HALF_ROPE, tm), lambda i: (i // nt, 0, i % nt)),
            pl.BlockSpec((1, tm, QK_ROPE_DIM), lambda i: (i // nt, i % nt, 0)),
            pl.BlockSpec((1, tm, QK_ROPE_DIM), lambda i: (i // nt, i % nt, 0)),
            _resident((D_MODEL, W_IN_COLS)),
            _resident((1, Q_LORA_RANK)),
            _resident((1, KV_LORA_RANK)),
            _resident((N_HEADS * QK_HEAD_DIM, Q_LORA_RANK)),
            _resident((KV_LORA_RANK, N_HEADS * QK_NOPE_DIM)),
            _resident((N_HEADS * V_HEAD_DIM, KV_LORA_RANK)),
            _resident((FOURIER_GROUPS, FOURIER_GROUP_DIM, 2 * FOURIER_GROUP_DIM)),
        ],
        out_specs=(
            pl.BlockSpec((1, 2, tm, D_FOURIER), lambda i: (i // nt, 0, i % nt, 0)),
            pl.BlockSpec((tm, D_FOURIER), row),
            pl.BlockSpec((tm, D_ATTN), row),
            pl.BlockSpec((1, N_HEADS, QK_PAD_DIM, tm), bh),
            pl.BlockSpec((1, N_HEADS, tm, QK_PAD_DIM), lambda i: (i // nt, 0, i % nt, 0)),
            pl.BlockSpec((1, N_HEADS, V_HEAD_DIM, tm), bh),
        ),
        compiler_params=pltpu.CompilerParams(
            dimension_semantics=("arbitrary",), vmem_limit_bytes=VMEM_LIMIT),
        name="token_stage",
    )(x2d, scale, shift, cos_t, sin_t, cos2, sin2,
      w["w_in"], w["q_norm"], w["kv_norm"], w["wq_t"], w["wk"], w["wv_t"], w["ab"])


def _seq_dft_matrix():
    r = 64
    k = np.arange(SEQ, dtype=np.int64)[:, None]
    j = np.arange(r, dtype=np.int64)[None, :]
    ang_hi = 2.0 * np.pi * ((k * j * r) % SEQ) / SEQ
    ang_lo = 2.0 * np.pi * ((k * j) % SEQ) / SEQ
    ch, sh = (jnp.asarray(f(ang_hi).astype(np.float32))[:, :, None] for f in (np.cos, np.sin))
    cl, sl = (jnp.asarray(f(ang_lo).astype(np.float32))[:, None, :] for f in (np.cos, np.sin))
    norm = 1.0 / math.sqrt(SEQ)
    cos_m = ((ch * cl - sh * sl) * norm).reshape(SEQ, SEQ)
    neg_sin_m = ((sh * cl + ch * sl) * (-norm)).reshape(SEQ, SEQ)
    return jnp.concatenate([cos_m, neg_sin_m], axis=1).astype(BF16)


def _seq_dft_kernel(cs_ref, pq_ref, g_ref, o_ref):
    y = jnp.dot(cs_ref[...], pq_ref[0], preferred_element_type=F32)
    o_ref[...] = (y * g_ref[...].astype(F32)).astype(BF16)


def _seq_dft(cs, pq, gate_f):
    tm, tn = 512, 512
    nm = SEQ // tm
    pq2 = pq.reshape(BATCH, 2 * SEQ, D_FOURIER)
    return pl.pallas_call(
        _seq_dft_kernel,
        out_shape=jax.ShapeDtypeStruct((TOKENS, D_FOURIER), BF16),
        grid=(BATCH, D_FOURIER // tn, nm),
        in_specs=[
            pl.BlockSpec((tm, 2 * SEQ), lambda b, n, m: (m, 0)),
            pl.BlockSpec((1, 2 * SEQ, tn), lambda b, n, m: (b, 0, n)),
            pl.BlockSpec((tm, tn), lambda b, n, m: (b * nm + m, n)),
        ],
        out_specs=pl.BlockSpec((tm, tn), lambda b, n, m: (b * nm + m, n)),
        compiler_params=pltpu.CompilerParams(
            dimension_semantics=("arbitrary", "arbitrary", "arbitrary"),
            vmem_limit_bytes=VMEM_LIMIT),
        name="seq_dft",
    )(cs, pq2, gate_f)


def _attn_kernel(q_t_ref, k_ref, v_t_ref, g_ref, o_ref, *, tk):
    q_t = q_t_ref[0, 0]
    tq = q_t.shape[1]
    m = jnp.full((1, tq), -jnp.inf, F32)
    l = jnp.zeros((1, tq), F32)
    acc = jnp.zeros((V_HEAD_DIM, tq), F32)
    for ci in range(SEQ // tk):
        s = jnp.dot(k_ref[0, 0, ci * tk:(ci + 1) * tk, :], q_t,
                    preferred_element_type=F32)
        m_new = jnp.maximum(m, jnp.max(s, axis=0, keepdims=True))
        alpha = jnp.exp2(m - m_new)
        p = jnp.exp2(s - m_new)
        l = alpha * l + jnp.sum(p, axis=0, keepdims=True)
        acc = alpha * acc + jnp.dot(v_t_ref[0, 0, :, ci * tk:(ci + 1) * tk], p.astype(BF16),
                                    preferred_element_type=F32)
        m = m_new
    o = (acc * (1.0 / l)).T
    o_ref[...] = (o * g_ref[...].astype(F32)).astype(BF16)


def _attention(q_t, k, v_t, gate_a):
    tq, tk = 512, 512
    nq = SEQ // tq
    return pl.pallas_call(
        functools.partial(_attn_kernel, tk=tk),
        out_shape=jax.ShapeDtypeStruct((TOKENS, D_ATTN), BF16),
        grid=(BATCH, N_HEADS, nq),
        in_specs=[
            pl.BlockSpec((1, 1, QK_PAD_DIM, tq), lambda b, h, i: (b, h, 0, i)),
            pl.BlockSpec((1, 1, SEQ, QK_PAD_DIM), lambda b, h, i: (b, h, 0, 0)),
            pl.BlockSpec((1, 1, V_HEAD_DIM, SEQ), lambda b, h, i: (b, h, 0, 0)),
            pl.BlockSpec((tq, V_HEAD_DIM), lambda b, h, i: (b * nq + i, h)),
        ],
        out_specs=pl.BlockSpec((tq, V_HEAD_DIM), lambda b, h, i: (b * nq + i, h)),
        compiler_params=pltpu.CompilerParams(
            dimension_semantics=("arbitrary", "arbitrary", "arbitrary"),
            vmem_limit_bytes=VMEM_LIMIT),
        name="attention",
    )(q_t, k, v_t, gate_a)


def _out_kernel(yf_ref, ya_ref, x_ref, gate_ref, w_ref, g_ref, b_ref, o_ref):
    y = (jnp.dot(yf_ref[...], w_ref[:D_FOURIER, :], preferred_element_type=F32)
         + jnp.dot(ya_ref[...], w_ref[D_FOURIER:, :], preferred_element_type=F32))
    r = DEEPNORM_ALPHA * x_ref[...] + gate_ref[0] * y
    mu = jnp.mean(r, axis=-1, keepdims=True)
    rc = r - mu
    var = jnp.mean(rc * rc, axis=-1, keepdims=True)
    o_ref[...] = rc * lax.rsqrt(var + NORM_EPS) * g_ref[...] + b_ref[...]


def _output_stage(y_f, y_a, x2d, gate, w_out_bf16, ln_g, ln_b):
    tm = 512
    nt = SEQ // tm
    row = lambda i: (i, 0)
    return pl.pallas_call(
        _out_kernel,
        out_shape=jax.ShapeDtypeStruct((TOKENS, D_MODEL), F32),
        grid=(TOKENS // tm,),
        in_specs=[
            pl.BlockSpec((tm, D_FOURIER), row),
            pl.BlockSpec((tm, D_ATTN), row),
            pl.BlockSpec((tm, D_MODEL), row),
            pl.BlockSpec((1, 1, D_MODEL), lambda i: (i // nt, 0, 0)),
            _resident((D_MODEL, D_MODEL)),
            _resident((1, D_MODEL)),
            _resident((1, D_MODEL)),
        ],
        out_specs=pl.BlockSpec((tm, D_MODEL), row),
        compiler_params=pltpu.CompilerParams(
            dimension_semantics=("arbitrary",), vmem_limit_bytes=VMEM_LIMIT),
        name="output_stage",
    )(y_f, y_a, x2d, gate, w_out_bf16, ln_g, ln_b)


def _layer_weights(w_in_l, q_norm_l, w_q_b_l, kv_norm_l, w_kv_b_l, w_fmix_l):
    u_f, z_f, cq, ckv, k_r, z_a = jnp.split(
        w_in_l, [1024, 2048, 2560, 2816, 2880], axis=1)
    pad = jnp.zeros((D_MODEL, W_IN_COLS - w_in_l.shape[1]), w_in_l.dtype)
    w_in = jnp.concatenate([u_f, z_f, z_a, cq, ckv, k_r, pad], axis=1).astype(BF16)
    wq = w_q_b_l.reshape(Q_LORA_RANK, N_HEADS, QK_HEAD_DIM)
    wq = jnp.concatenate([
        wq[:, :, :QK_NOPE_DIM].reshape(Q_LORA_RANK, -1),
        wq[:, :, QK_NOPE_DIM:QK_NOPE_DIM + HALF_ROPE].reshape(Q_LORA_RANK, -1),
        wq[:, :, QK_NOPE_DIM + HALF_ROPE:].reshape(Q_LORA_RANK, -1)], axis=1)
    wkv = w_kv_b_l.reshape(KV_LORA_RANK, N_HEADS, QK_NOPE_DIM + V_HEAD_DIM)
    wk = wkv[:, :, :QK_NOPE_DIM].reshape(KV_LORA_RANK, -1)
    wv = wkv[:, :, QK_NOPE_DIM:].reshape(KV_LORA_RANK, -1)
    return {
        "w_in": w_in,
        "q_norm": q_norm_l.reshape(1, Q_LORA_RANK),
        "kv_norm": kv_norm_l.reshape(1, KV_LORA_RANK),
        "wq_t": wq.T.astype(BF16),
        "wk": wk.astype(BF16),
        "wv_t": wv.T.astype(BF16),
        "ab": _fold_fourier_weights(w_fmix_l),
    }


def kernel(x, c, positions, w_ada, b_ada, w_in, q_norm, w_q_b, kv_norm, w_kv_b, w_fmix, w_out,
           ln_g, ln_b):
    assert x.shape == (BATCH, SEQ, D_MODEL) and w_ada.shape[0] == DEPTH
    mod = _ada_modulation(c, w_ada, b_ada)
    tables = _rope_tables(positions)
    cs = _seq_dft_matrix()
    x2d = x.reshape(TOKENS, D_MODEL)
    for l in range(DEPTH):
        shift, scale, gate = (
            mod[l, :BATCH, i * D_MODEL:(i + 1) * D_MODEL].reshape(BATCH, 1, D_MODEL)
            for i in range(3))
        w = _layer_weights(w_in[l], q_norm[l], w_q_b[l], kv_norm[l], w_kv_b[l], w_fmix[l])
        pq, gate_f, gate_a, q_t, k, v_t = _token_stage(x2d, scale, shift, tables, w)
        y_f = _seq_dft(cs, pq, gate_f)
        y_a = _attention(q_t, k, v_t, gate_a)
        x2d = _output_stage(y_f, y_a, x2d, gate, w_out[l].astype(BF16),
                            ln_g[l].reshape(1, D_MODEL), ln_b[l].reshape(1, D_MODEL))
    return x2d.reshape(BATCH, SEQ, D_MODEL)
```

```python
import functools
import math

import numpy as np
import jax
import jax.numpy as jnp
from jax import lax
from jax.experimental import pallas as pl
from jax.experimental.pallas import tpu as pltpu

D_MODEL = 2048
BATCH = 4
SEQ = 4096
DEPTH = 2
D_FOURIER = D_MODEL // 2
FOURIER_GROUPS = 8
FOURIER_GROUP_DIM = D_FOURIER // FOURIER_GROUPS
V_HEAD_DIM = 128
D_ATTN = D_MODEL // 2
N_HEADS = D_ATTN // V_HEAD_DIM
QK_NOPE_DIM = 128
QK_ROPE_DIM = 64
QK_HEAD_DIM = QK_NOPE_DIM + QK_ROPE_DIM
Q_LORA_RANK = D_MODEL // 4
KV_LORA_RANK = D_MODEL // 8
ROPE_THETA = 10000.0
NORM_EPS = 1e-6
DEEPNORM_ALPHA = (2 * DEPTH) ** 0.25

TOKENS = BATCH * SEQ
HALF_ROPE = QK_ROPE_DIM // 2
QK_PAD_DIM = 256
W_IN_COLS = 3968
COL_U, COL_ZF, COL_ZA, COL_CQ, COL_CKV = 0, 1024, 2048, 3072, 3584

F32 = jnp.float32
BF16 = jnp.bfloat16
VMEM_LIMIT = 52 * 1024 * 1024

Q_SCALE = math.log2(math.e) / math.sqrt(QK_HEAD_DIM)


def _silu(v):
    return v * (1.0 / (1.0 + jnp.exp(-v)))


def _nt_dot(a, b):
    return lax.dot_general(a, b, (((1,), (1,)), ((), ())), preferred_element_type=F32)


def _ada_kernel(c_ref, w_ref, b_ref, o_ref):
    c_act = _silu(c_ref[...]).astype(BF16)
    w = w_ref[0].astype(BF16)
    o_ref[0] = jnp.dot(c_act, w, preferred_element_type=F32) + b_ref[0]


def _ada_modulation(c, w_ada, b_ada):
    tn = 768
    c_pad = jnp.pad(c, ((0, 8 - BATCH), (0, 0)))
    b3 = b_ada.reshape(DEPTH, 1, 3 * D_MODEL)
    return pl.pallas_call(
        _ada_kernel,
        out_shape=jax.ShapeDtypeStruct((DEPTH, 8, 3 * D_MODEL), F32),
        grid=(DEPTH, 3 * D_MODEL // tn),
        in_specs=[
            pl.BlockSpec((8, D_MODEL), lambda l, j: (0, 0)),
            pl.BlockSpec((1, D_MODEL, tn), lambda l, j: (l, 0, j)),
            pl.BlockSpec((1, 1, tn), lambda l, j: (l, 0, j)),
        ],
        out_specs=pl.BlockSpec((1, 8, tn), lambda l, j: (l, 0, j)),
        compiler_params=pltpu.CompilerParams(
            dimension_semantics=("arbitrary", "arbitrary"), vmem_limit_bytes=VMEM_LIMIT),
        name="ada_modulation",
    )(c_pad, w_ada, b3)


def _rope_kernel(pos_row_ref, pos_col_ref, invf_col_ref, invf_row_ref,
                 cos_t_ref, sin_t_ref, cos2_ref, sin2_ref):
    ang_t = invf_col_ref[...] * pos_row_ref[0].astype(F32)
    cos_t_ref[0] = jnp.cos(ang_t)
    sin_t_ref[0] = jnp.sin(ang_t)
    ang2 = pos_col_ref[0].astype(F32) * invf_row_ref[...]
    lane = lax.broadcasted_iota(jnp.int32, ang2.shape, 1)
    cos2_ref[0] = jnp.cos(ang2)
    sin2_ref[0] = jnp.where(lane < HALF_ROPE, -jnp.sin(ang2), jnp.sin(ang2))


def _rope_tables(positions):
    ts = 1024
    inv_freq = ROPE_THETA ** (-jnp.arange(0, QK_ROPE_DIM, 2, dtype=F32) / QK_ROPE_DIM)
    invf_col = inv_freq.reshape(HALF_ROPE, 1)
    invf_row = jnp.concatenate([inv_freq, inv_freq]).reshape(1, QK_ROPE_DIM)
    pos_row = positions.reshape(BATCH, 1, SEQ)
    pos_col = positions.reshape(BATCH, SEQ, 1)
    return pl.pallas_call(
        _rope_kernel,
        out_shape=(
            jax.ShapeDtypeStruct((BATCH, HALF_ROPE, SEQ), F32),
            jax.ShapeDtypeStruct((BATCH, HALF_ROPE, SEQ), F32),
            jax.ShapeDtypeStruct((BATCH, SEQ, QK_ROPE_DIM), F32),
            jax.ShapeDtypeStruct((BATCH, SEQ, QK_ROPE_DIM), F32),
        ),
        grid=(BATCH, SEQ // ts),
        in_specs=[
            pl.BlockSpec((1, 1, ts), lambda b, s: (b, 0, s)),
            pl.BlockSpec((1, ts, 1), lambda b, s: (b, s, 0)),
            pl.BlockSpec((HALF_ROPE, 1), lambda b, s: (0, 0)),
            pl.BlockSpec((1, QK_ROPE_DIM), lambda b, s: (0, 0)),
        ],
        out_specs=(
            pl.BlockSpec((1, HALF_ROPE, ts), lambda b, s: (b, 0, s)),
            pl.BlockSpec((1, HALF_ROPE, ts), lambda b, s: (b, 0, s)),
            pl.BlockSpec((1, ts, QK_ROPE_DIM), lambda b, s: (b, s, 0)),
            pl.BlockSpec((1, ts, QK_ROPE_DIM), lambda b, s: (b, s, 0)),
        ),
        compiler_params=pltpu.CompilerParams(
            dimension_semantics=("arbitrary", "arbitrary"), vmem_limit_bytes=VMEM_LIMIT),
        name="rope_tables",
    )(pos_row, pos_col, invf_col, invf_row)


def _channel_dft_matrix():
    idx = np.arange(FOURIER_GROUP_DIM)
    ang = 2.0 * np.pi * ((idx[:, None] * idx[None, :]) % FOURIER_GROUP_DIM) / FOURIER_GROUP_DIM
    scale = 1.0 / math.sqrt(FOURIER_GROUP_DIM)
    return np.concatenate([np.cos(ang), np.sin(ang)], axis=1).astype(np.float32) * np.float32(scale)


def _fold_kernel(cs_ref, w_ref, o_ref):
    w = w_ref[0]
    dims = (((1,), (0,)), ((), ()))
    a = lax.dot_general(cs_ref[:, :FOURIER_GROUP_DIM], w, dims,
                        precision=lax.Precision.HIGHEST, preferred_element_type=F32)
    b = lax.dot_general(cs_ref[:, FOURIER_GROUP_DIM:], w, dims,
                        precision=lax.Precision.HIGHEST, preferred_element_type=F32)
    o_ref[0] = jnp.concatenate([a, b], axis=1).astype(BF16)


def _fold_fourier_weights(w_fmix_l):
    g, c = FOURIER_GROUPS, FOURIER_GROUP_DIM
    return pl.pallas_call(
        _fold_kernel,
        out_shape=jax.ShapeDtypeStruct((g, c, 2 * c), BF16),
        grid=(g,),
        in_specs=[
            pl.BlockSpec((c, 2 * c), lambda i: (0, 0)),
            pl.BlockSpec((1, c, c), lambda i: (i, 0, 0)),
        ],
        out_specs=pl.BlockSpec((1, c, 2 * c), lambda i: (i, 0, 0)),
        compiler_params=pltpu.CompilerParams(dimension_semantics=("arbitrary",)),
        name="fold_fourier_weights",
    )(jnp.asarray(_channel_dft_matrix()), w_fmix_l)


def _token_kernel(x_ref, scale_ref, shift_ref, cos_t_ref, sin_t_ref, cos2_ref, sin2_ref,
                  w_in_ref, qn_ref, kvn_ref, wq_t_ref, wk_ref, wv_t_ref, ab_ref,
                  pq_ref, gf_ref, ga_ref, q_t_ref, k_ref, v_t_ref):
    x = x_ref[...]
    tm = x.shape[0]
    mu = jnp.mean(x, axis=-1, keepdims=True)
    xc = x - mu
    var = jnp.mean(xc * xc, axis=-1, keepdims=True)
    h = xc * lax.rsqrt(var + NORM_EPS) * (1.0 + scale_ref[0]) + shift_ref[0]
    hb = h.astype(BF16)

    def proj(lo, hi):
        return jnp.dot(hb, w_in_ref[:, lo:hi], preferred_element_type=F32)

    ub = proj(COL_U, COL_ZF).astype(BF16)
    c = FOURIER_GROUP_DIM
    for g in range(FOURIER_GROUPS):
        pq = jnp.dot(ub[:, g * c:(g + 1) * c], ab_ref[g], preferred_element_type=F32)
        pq_ref[0, 0, :, g * c:(g + 1) * c] = pq[:, :c].astype(BF16)
        pq_ref[0, 1, :, g * c:(g + 1) * c] = pq[:, c:].astype(BF16)

    gf_ref[...] = _silu(proj(COL_ZF, COL_ZA)).astype(BF16)
    ga_ref[...] = _silu(proj(COL_ZA, COL_CQ)).astype(BF16)

    cq = proj(COL_CQ, COL_CKV)
    cqn = (cq * lax.rsqrt(jnp.mean(cq * cq, axis=-1, keepdims=True) + NORM_EPS)
           * qn_ref[...]).astype(BF16)
    q_t = _nt_dot(wq_t_ref[...], cqn) * Q_SCALE
    cos_t = cos_t_ref[0]
    sin_t = sin_t_ref[0]
    n_nope = N_HEADS * QK_NOPE_DIM
    n_half = N_HEADS * HALF_ROPE
    zeros_q = jnp.zeros((QK_PAD_DIM - QK_HEAD_DIM, tm), BF16)
    for hd in range(N_HEADS):
        q_t_ref[0, hd, 0:QK_NOPE_DIM, :] = (
            q_t[hd * QK_NOPE_DIM:(hd + 1) * QK_NOPE_DIM].astype(BF16))
        x1 = q_t[n_nope + hd * HALF_ROPE:n_nope + (hd + 1) * HALF_ROPE]
        x2 = q_t[n_nope + n_half + hd * HALF_ROPE:n_nope + n_half + (hd + 1) * HALF_ROPE]
        q_t_ref[0, hd, QK_NOPE_DIM:QK_NOPE_DIM + HALF_ROPE, :] = (
            x1 * cos_t - x2 * sin_t).astype(BF16)
        q_t_ref[0, hd, QK_NOPE_DIM + HALF_ROPE:QK_HEAD_DIM, :] = (
            x2 * cos_t + x1 * sin_t).astype(BF16)
        q_t_ref[0, hd, QK_HEAD_DIM:QK_PAD_DIM, :] = zeros_q

    ckv_kr = proj(COL_CKV, W_IN_COLS)
    ckv = ckv_kr[:, :KV_LORA_RANK]
    ckvn = (ckv * lax.rsqrt(jnp.mean(ckv * ckv, axis=-1, keepdims=True) + NORM_EPS)
            * kvn_ref[...]).astype(BF16)
    k_nope = jnp.dot(ckvn, wk_ref[...], preferred_element_type=F32)
    v_t = _nt_dot(wv_t_ref[...], ckvn)
    kr = ckv_kr[:, KV_LORA_RANK:KV_LORA_RANK + QK_ROPE_DIM]
    kr_swapped = jnp.concatenate([kr[:, HALF_ROPE:], kr[:, :HALF_ROPE]], axis=1)
    kr_rot = (kr * cos2_ref[0] + kr_swapped * sin2_ref[0]).astype(BF16)
    kr_pad = jnp.concatenate(
        [kr_rot, jnp.zeros((tm, QK_PAD_DIM - QK_HEAD_DIM), BF16)], axis=1)
    for hd in range(N_HEADS):
        k_ref[0, hd, :, 0:QK_NOPE_DIM] = (
            k_nope[:, hd * QK_NOPE_DIM:(hd + 1) * QK_NOPE_DIM].astype(BF16))
        k_ref[0, hd, :, QK_NOPE_DIM:QK_PAD_DIM] = kr_pad
        v_t_ref[0, hd] = v_t[hd * V_HEAD_DIM:(hd + 1) * V_HEAD_DIM].astype(BF16)


def _resident(shape):
    zeros = (0,) * len(shape)
    return pl.BlockSpec(shape, lambda *_: zeros, pipeline_mode=pl.Buffered(1))


def _token_stage(x2d, scale, shift, tables, w):
    tm = 256
    nt = SEQ // tm
    cos_t, sin_t, cos2, sin2 = tables
    row = lambda i: (i, 0)
    per_batch = lambda i: (i // nt, 0, 0)
    bh = lambda i: (i // nt, 0, 0, i % nt)
    return pl.pallas_call(
        _token_kernel,
        out_shape=(
            jax.ShapeDtypeStruct((BATCH, 2, SEQ, D_FOURIER), BF16),
            jax.ShapeDtypeStruct((TOKENS, D_FOURIER), BF16),
            jax.ShapeDtypeStruct((TOKENS, D_ATTN), BF16),
            jax.ShapeDtypeStruct((BATCH, N_HEADS, QK_PAD_DIM, SEQ), BF16),
            jax.ShapeDtypeStruct((BATCH, N_HEADS, SEQ, QK_PAD_DIM), BF16),
            jax.ShapeDtypeStruct((BATCH, N_HEADS, V_HEAD_DIM, SEQ), BF16),
        ),
        grid=(TOKENS // tm,),
        in_specs=[
            pl.BlockSpec((tm, D_MODEL), row),
            pl.BlockSpec((1, 1, D_MODEL), per_batch),
            pl.BlockSpec((1, 1, D_MODEL), per_batch),
            pl.BlockSpec((1, HALF_ROPE, tm), lambda i: (i // nt, 0, i % nt)),
            pl.BlockSpec((1, HALF_ROPE, tm), lambda i: (i // nt, 0, i % nt)),
            pl.BlockSpec((1, tm, QK_ROPE_DIM), lambda i: (i // nt, i % nt, 0)),
            pl.BlockSpec((1, tm, QK_ROPE_DIM), lambda i: (i // nt, i % nt, 0)),
            _resident((D_MODEL, W_IN_COLS)),
            _resident((1, Q_LORA_RANK)),
            _resident((1, KV_LORA_RANK)),
            _resident((N_HEADS * QK_HEAD_DIM, Q_LORA_RANK)),
            _resident((KV_LORA_RANK, N_HEADS * QK_NOPE_DIM)),
            _resident((N_HEADS * V_HEAD_DIM, KV_LORA_RANK)),
            _resident((FOURIER_GROUPS, FOURIER_GROUP_DIM, 2 * FOURIER_GROUP_DIM)),
        ],
        out_specs=(
            pl.BlockSpec((1, 2, tm, D_FOURIER), lambda i: (i // nt, 0, i % nt, 0)),
            pl.BlockSpec((tm, D_FOURIER), row),
            pl.BlockSpec((tm, D_ATTN), row),
            pl.BlockSpec((1, N_HEADS, QK_PAD_DIM, tm), bh),
            pl.BlockSpec((1, N_HEADS, tm, QK_PAD_DIM), lambda i: (i // nt, 0, i % nt, 0)),
            pl.BlockSpec((1, N_HEADS, V_HEAD_DIM, tm), bh),
        ),
        compiler_params=pltpu.CompilerParams(
            dimension_semantics=("arbitrary",), vmem_limit_bytes=VMEM_LIMIT),
        name="token_stage",
    )(x2d, scale, shift, cos_t, sin_t, cos2, sin2,
      w["w_in"], w["q_norm"], w["kv_norm"], w["wq_t"], w["wk"], w["wv_t"], w["ab"])


def _seq_dft_matrix():
    r = 64
    k = np.arange(SEQ, dtype=np.int64)[:, None]
    j = np.arange(r, dtype=np.int64)[None, :]
    ang_hi = 2.0 * np.pi * ((k * j * r) % SEQ) / SEQ
    ang_lo = 2.0 * np.pi * ((k * j) % SEQ) / SEQ
    ch, sh = (jnp.asarray(f(ang_hi).astype(np.float32))[:, :, None] for f in (np.cos, np.sin))
    cl, sl = (jnp.asarray(f(ang_lo).astype(np.float32))[:, None, :] for f in (np.cos, np.sin))
    norm = 1.0 / math.sqrt(SEQ)
    cos_m = ((ch * cl - sh * sl) * norm).reshape(SEQ, SEQ)
    neg_sin_m = ((sh * cl + ch * sl) * (-norm)).reshape(SEQ, SEQ)
    return jnp.concatenate([cos_m, neg_sin_m], axis=1).astype(BF16)


def _seq_dft_kernel(cs_ref, pq_ref, g_ref, o_ref):
    y = jnp.dot(cs_ref[...], pq_ref[0], preferred_element_type=F32)
    o_ref[...] = (y * g_ref[...].astype(F32)).astype(BF16)


def _seq_dft(cs, pq, gate_f):
    tm, tn = 512, 512
    nm = SEQ // tm
    pq2 = pq.reshape(BATCH, 2 * SEQ, D_FOURIER)
    return pl.pallas_call(
        _seq_dft_kernel,
        out_shape=jax.ShapeDtypeStruct((TOKENS, D_FOURIER), BF16),
        grid=(BATCH, D_FOURIER // tn, nm),
        in_specs=[
            pl.BlockSpec((tm, 2 * SEQ), lambda b, n, m: (m, 0)),
            pl.BlockSpec((1, 2 * SEQ, tn), lambda b, n, m: (b, 0, n)),
            pl.BlockSpec((tm, tn), lambda b, n, m: (b * nm + m, n)),
        ],
        out_specs=pl.BlockSpec((tm, tn), lambda b, n, m: (b * nm + m, n)),
        compiler_params=pltpu.CompilerParams(
            dimension_semantics=("arbitrary", "arbitrary", "arbitrary"),
            vmem_limit_bytes=VMEM_LIMIT),
        name="seq_dft",
    )(cs, pq2, gate_f)


def _attn_kernel(q_t_ref, k_ref, v_t_ref, g_ref, o_ref, s_scr, p_scr, acc_scr, *, tk):
    tq = q_t_ref.shape[3]
    n_chunks = SEQ // tk

    def scores(c, slot):
        off = pl.multiple_of(c * tk, tk)
        s = jnp.dot(k_ref[0, 0, pl.ds(off, tk), :], q_t_ref[0, 0],
                    preferred_element_type=F32)
        s_scr[slot] = s
        return jnp.max(s, axis=0, keepdims=True)

    def softmax(slot, m_chunk, m, l):
        m_new = jnp.maximum(m, m_chunk)
        alpha = jnp.exp2(m - m_new)
        p = jnp.exp2(s_scr[slot] - m_new)
        p_scr[slot] = p.astype(BF16)
        return m_new, alpha * l + jnp.sum(p, axis=0, keepdims=True), alpha

    def values(c, slot, alpha):
        off = pl.multiple_of(c * tk, tk)
        acc_scr[...] = alpha * acc_scr[...] + jnp.dot(
            v_t_ref[0, 0, :, pl.ds(off, tk)], p_scr[slot], preferred_element_type=F32)

    def step(c, slot, carry):
        m, l, alpha_prev, m_chunk = carry
        m_chunk_next = scores(c + 1, 1 - slot)
        m, l, alpha = softmax(slot, m_chunk, m, l)
        values(c - 1, 1 - slot, alpha_prev)
        return m, l, alpha, m_chunk_next

    acc_scr[...] = jnp.zeros_like(acc_scr)
    m_chunk = scores(0, 0)
    m_chunk_next = scores(1, 1)
    m, l, alpha = softmax(0, m_chunk, jnp.full((1, tq), -jnp.inf, F32), jnp.zeros((1, tq), F32))
    carry = (m, l, alpha, m_chunk_next)

    def pair(j, carry):
        c = 1 + 2 * j
        return step(c + 1, 0, step(c, 1, carry))

    for j in range((n_chunks - 2) // 2):
        carry = pair(j, carry)
    m, l, alpha_prev, m_chunk = carry
    last = n_chunks - 1
    m, l, alpha = softmax(1, m_chunk, m, l)
    values(last - 1, 0, alpha_prev)
    values(last, 1, alpha)
    o = (acc_scr[...] * (1.0 / l)).T
    o_ref[...] = (o * g_ref[...].astype(F32)).astype(BF16)


def _attention(q_t, k, v_t, gate_a):
    tq, tk = 512, 512
    nq = SEQ // tq
    assert (SEQ // tk) % 2 == 0
    return pl.pallas_call(
        functools.partial(_attn_kernel, tk=tk),
        scratch_shapes=[
            pltpu.VMEM((2, tk, tq), F32),
            pltpu.VMEM((2, tk, tq), BF16),
            pltpu.VMEM((V_HEAD_DIM, tq), F32),
        ],
        out_shape=jax.ShapeDtypeStruct((TOKENS, D_ATTN), BF16),
        grid=(BATCH, N_HEADS, nq),
        in_specs=[
            pl.BlockSpec((1, 1, QK_PAD_DIM, tq), lambda b, h, i: (b, h, 0, i)),
            pl.BlockSpec((1, 1, SEQ, QK_PAD_DIM), lambda b, h, i: (b, h, 0, 0)),
            pl.BlockSpec((1, 1, V_HEAD_DIM, SEQ), lambda b, h, i: (b, h, 0, 0)),
            pl.BlockSpec((tq, V_HEAD_DIM), lambda b, h, i: (b * nq + i, h)),
        ],
        out_specs=pl.BlockSpec((tq, V_HEAD_DIM), lambda b, h, i: (b * nq + i, h)),
        compiler_params=pltpu.CompilerParams(
            dimension_semantics=("arbitrary", "arbitrary", "arbitrary"),
            vmem_limit_bytes=VMEM_LIMIT),
        name="attention",
    )(q_t, k, v_t, gate_a)


def _out_kernel(yf_ref, ya_ref, x_ref, gate_ref, w_ref, g_ref, b_ref, o_ref):
    y = (jnp.dot(yf_ref[...], w_ref[:D_FOURIER, :], preferred_element_type=F32)
         + jnp.dot(ya_ref[...], w_ref[D_FOURIER:, :], preferred_element_type=F32))
    r = DEEPNORM_ALPHA * x_ref[...] + gate_ref[0] * y
    mu = jnp.mean(r, axis=-1, keepdims=True)
    rc = r - mu
    var = jnp.mean(rc * rc, axis=-1, keepdims=True)
    o_ref[...] = rc * lax.rsqrt(var + NORM_EPS) * g_ref[...] + b_ref[...]


def _output_stage(y_f, y_a, x2d, gate, w_out_bf16, ln_g, ln_b):
    tm = 512
    nt = SEQ // tm
    row = lambda i: (i, 0)
    return pl.pallas_call(
        _out_kernel,
        out_shape=jax.ShapeDtypeStruct((TOKENS, D_MODEL), F32),
        grid=(TOKENS // tm,),
        in_specs=[
            pl.BlockSpec((tm, D_FOURIER), row),
            pl.BlockSpec((tm, D_ATTN), row),
            pl.BlockSpec((tm, D_MODEL), row),
            pl.BlockSpec((1, 1, D_MODEL), lambda i: (i // nt, 0, 0)),
            _resident((D_MODEL, D_MODEL)),
            _resident((1, D_MODEL)),
            _resident((1, D_MODEL)),
        ],
        out_specs=pl.BlockSpec((tm, D_MODEL), row),
        compiler_params=pltpu.CompilerParams(
            dimension_semantics=("arbitrary",), vmem_limit_bytes=VMEM_LIMIT),
        name="output_stage",
    )(y_f, y_a, x2d, gate, w_out_bf16, ln_g, ln_b)


def _layer_weights(w_in_l, q_norm_l, w_q_b_l, kv_norm_l, w_kv_b_l, w_fmix_l):
    u_f, z_f, cq, ckv, k_r, z_a = jnp.split(
        w_in_l, [1024, 2048, 2560, 2816, 2880], axis=1)
    pad = jnp.zeros((D_MODEL, W_IN_COLS - w_in_l.shape[1]), w_in_l.dtype)
    w_in = jnp.concatenate([u_f, z_f, z_a, cq, ckv, k_r, pad], axis=1).astype(BF16)
    wq = w_q_b_l.reshape(Q_LORA_RANK, N_HEADS, QK_HEAD_DIM)
    wq = jnp.concatenate([
        wq[:, :, :QK_NOPE_DIM].reshape(Q_LORA_RANK, -1),
        wq[:, :, QK_NOPE_DIM:QK_NOPE_DIM + HALF_ROPE].reshape(Q_LORA_RANK, -1),
        wq[:, :, QK_NOPE_DIM + HALF_ROPE:].reshape(Q_LORA_RANK, -1)], axis=1)
    wkv = w_kv_b_l.reshape(KV_LORA_RANK, N_HEADS, QK_NOPE_DIM + V_HEAD_DIM)
    wk = wkv[:, :, :QK_NOPE_DIM].reshape(KV_LORA_RANK, -1)
    wv = wkv[:, :, QK_NOPE_DIM:].reshape(KV_LORA_RANK, -1)
    return {
        "w_in": w_in,
        "q_norm": q_norm_l.reshape(1, Q_LORA_RANK),
        "kv_norm": kv_norm_l.reshape(1, KV_LORA_RANK),
        "wq_t": wq.T.astype(BF16),
        "wk": wk.astype(BF16),
        "wv_t": wv.T.astype(BF16),
        "ab": _fold_fourier_weights(w_fmix_l),
    }


def kernel(x, c, positions, w_ada, b_ada, w_in, q_norm, w_q_b, kv_norm, w_kv_b, w_fmix, w_out,
           ln_g, ln_b):
    assert x.shape == (BATCH, SEQ, D_MODEL) and w_ada.shape[0] == DEPTH
    mod = _ada_modulation(c, w_ada, b_ada)
    tables = _rope_tables(positions)
    cs = _seq_dft_matrix()
    x2d = x.reshape(TOKENS, D_MODEL)
    for l in range(DEPTH):
        shift, scale, gate = (
            mod[l, :BATCH, i * D_MODEL:(i + 1) * D_MODEL].reshape(BATCH, 1, D_MODEL)
            for i in range(3))
        w = _layer_weights(w_in[l], q_norm[l], w_q_b[l], kv_norm[l], w_kv_b[l], w_fmix[l])
        pq, gate_f, gate_a, q_t, k, v_t = _token_stage(x2d, scale, shift, tables, w)
        y_f = _seq_dft(cs, pq, gate_f)
        y_a = _attention(q_t, k, v_t, gate_a)
        x2d = _output_stage(y_f, y_a, x2d, gate, w_out[l].astype(BF16),
                            ln_g[l].reshape(1, D_MODEL), ln_b[l].reshape(1, D_MODEL))
    return x2d.reshape(BATCH, SEQ, D_MODEL)
```

```python
import functools
import math

import numpy as np
import jax
import jax.numpy as jnp
from jax import lax
from jax.experimental import pallas as pl
from jax.experimental.pallas import tpu as pltpu

D_MODEL = 2048
BATCH = 4
SEQ = 4096
DEPTH = 2
D_FOURIER = D_MODEL // 2
FOURIER_GROUPS = 8
FOURIER_GROUP_DIM = D_FOURIER // FOURIER_GROUPS
V_HEAD_DIM = 128
D_ATTN = D_MODEL // 2
N_HEADS = D_ATTN // V_HEAD_DIM
QK_NOPE_DIM = 128
QK_ROPE_DIM = 64
QK_HEAD_DIM = QK_NOPE_DIM + QK_ROPE_DIM
Q_LORA_RANK = D_MODEL // 4
KV_LORA_RANK = D_MODEL // 8
ROPE_THETA = 10000.0
NORM_EPS = 1e-6
DEEPNORM_ALPHA = (2 * DEPTH) ** 0.25

TOKENS = BATCH * SEQ
HALF_ROPE = QK_ROPE_DIM // 2
QK_PAD_DIM = 256
W_IN_COLS = 3968
COL_U, COL_ZF, COL_ZA, COL_CQ, COL_CKV = 0, 1024, 2048, 3072, 3584

F32 = jnp.float32
BF16 = jnp.bfloat16
VMEM_LIMIT = 52 * 1024 * 1024

Q_SCALE = math.log2(math.e) / math.sqrt(QK_HEAD_DIM)


def _silu(v):
    return v * (1.0 / (1.0 + jnp.exp(-v)))


def _nt_dot(a, b):
    return lax.dot_general(a, b, (((1,), (1,)), ((), ())), preferred_element_type=F32)


def _ada_kernel(c_ref, w_ref, b_ref, o_ref):
    c_act = _silu(c_ref[...]).astype(BF16)
    w = w_ref[0].astype(BF16)
    o_ref[0] = jnp.dot(c_act, w, preferred_element_type=F32) + b_ref[0]


def _ada_modulation(c, w_ada, b_ada):
    tn = 768
    c_pad = jnp.pad(c, ((0, 8 - BATCH), (0, 0)))
    b3 = b_ada.reshape(DEPTH, 1, 3 * D_MODEL)
    return pl.pallas_call(
        _ada_kernel,
        out_shape=jax.ShapeDtypeStruct((DEPTH, 8, 3 * D_MODEL), F32),
        grid=(DEPTH, 3 * D_MODEL // tn),
        in_specs=[
            pl.BlockSpec((8, D_MODEL), lambda l, j: (0, 0)),
            pl.BlockSpec((1, D_MODEL, tn), lambda l, j: (l, 0, j)),
            pl.BlockSpec((1, 1, tn), lambda l, j: (l, 0, j)),
        ],
        out_specs=pl.BlockSpec((1, 8, tn), lambda l, j: (l, 0, j)),
        compiler_params=pltpu.CompilerParams(
            dimension_semantics=("arbitrary", "arbitrary"), vmem_limit_bytes=VMEM_LIMIT),
        name="ada_modulation",
    )(c_pad, w_ada, b3)


def _rope_kernel(pos_row_ref, pos_col_ref, invf_col_ref, invf_row_ref,
                 cos_t_ref, sin_t_ref, cos2_ref, sin2_ref):
    ang_t = invf_col_ref[...] * pos_row_ref[0].astype(F32)
    cos_t_ref[0] = jnp.cos(ang_t)
    sin_t_ref[0] = jnp.sin(ang_t)
    ang2 = pos_col_ref[0].astype(F32) * invf_row_ref[...]
    lane = lax.broadcasted_iota(jnp.int32, ang2.shape, 1)
    cos2_ref[0] = jnp.cos(ang2)
    sin2_ref[0] = jnp.where(lane < HALF_ROPE, -jnp.sin(ang2), jnp.sin(ang2))


def _rope_tables(positions):
    ts = 1024
    inv_freq = ROPE_THETA ** (-jnp.arange(0, QK_ROPE_DIM, 2, dtype=F32) / QK_ROPE_DIM)
    invf_col = inv_freq.reshape(HALF_ROPE, 1)
    invf_row = jnp.concatenate([inv_freq, inv_freq]).reshape(1, QK_ROPE_DIM)
    pos_row = positions.reshape(BATCH, 1, SEQ)
    pos_col = positions.reshape(BATCH, SEQ, 1)
    return pl.pallas_call(
        _rope_kernel,
        out_shape=(
            jax.ShapeDtypeStruct((BATCH, HALF_ROPE, SEQ), F32),
            jax.ShapeDtypeStruct((BATCH, HALF_ROPE, SEQ), F32),
            jax.ShapeDtypeStruct((BATCH, SEQ, QK_ROPE_DIM), F32),
            jax.ShapeDtypeStruct((BATCH, SEQ, QK_ROPE_DIM), F32),
        ),
        grid=(BATCH, SEQ // ts),
        in_specs=[
            pl.BlockSpec((1, 1, ts), lambda b, s: (b, 0, s)),
            pl.BlockSpec((1, ts, 1), lambda b, s: (b, s, 0)),
            pl.BlockSpec((HALF_ROPE, 1), lambda b, s: (0, 0)),
            pl.BlockSpec((1, QK_ROPE_DIM), lambda b, s: (0, 0)),
        ],
        out_specs=(
            pl.BlockSpec((1, HALF_ROPE, ts), lambda b, s: (b, 0, s)),
            pl.BlockSpec((1, HALF_ROPE, ts), lambda b, s: (b, 0, s)),
            pl.BlockSpec((1, ts, QK_ROPE_DIM), lambda b, s: (b, s, 0)),
            pl.BlockSpec((1, ts, QK_ROPE_DIM), lambda b, s: (b, s, 0)),
        ),
        compiler_params=pltpu.CompilerParams(
            dimension_semantics=("arbitrary", "arbitrary"), vmem_limit_bytes=VMEM_LIMIT),
        name="rope_tables",
    )(pos_row, pos_col, invf_col, invf_row)


def _channel_dft_matrix():
    idx = np.arange(FOURIER_GROUP_DIM)
    ang = 2.0 * np.pi * ((idx[:, None] * idx[None, :]) % FOURIER_GROUP_DIM) / FOURIER_GROUP_DIM
    scale = 1.0 / math.sqrt(FOURIER_GROUP_DIM)
    return np.concatenate([np.cos(ang), np.sin(ang)], axis=1).astype(np.float32) * np.float32(scale)


def _fold_kernel(cs_ref, w_ref, o_ref):
    w = w_ref[0]
    dims = (((1,), (0,)), ((), ()))
    a = lax.dot_general(cs_ref[:, :FOURIER_GROUP_DIM], w, dims,
                        precision=lax.Precision.HIGHEST, preferred_element_type=F32)
    b = lax.dot_general(cs_ref[:, FOURIER_GROUP_DIM:], w, dims,
                        precision=lax.Precision.HIGHEST, preferred_element_type=F32)
    o_ref[0] = jnp.concatenate([a, b], axis=1).astype(BF16)


def _fold_fourier_weights(w_fmix_l):
    g, c = FOURIER_GROUPS, FOURIER_GROUP_DIM
    return pl.pallas_call(
        _fold_kernel,
        out_shape=jax.ShapeDtypeStruct((g, c, 2 * c), BF16),
        grid=(g,),
        in_specs=[
            pl.BlockSpec((c, 2 * c), lambda i: (0, 0)),
            pl.BlockSpec((1, c, c), lambda i: (i, 0, 0)),
        ],
        out_specs=pl.BlockSpec((1, c, 2 * c), lambda i: (i, 0, 0)),
        compiler_params=pltpu.CompilerParams(dimension_semantics=("arbitrary",)),
        name="fold_fourier_weights",
    )(jnp.asarray(_channel_dft_matrix()), w_fmix_l)


def _token_kernel(x_ref, scale_ref, shift_ref, cos_t_ref, sin_t_ref, cos2_ref, sin2_ref,
                  w_in_ref, qn_ref, kvn_ref, wq_t_ref, wk_ref, wv_t_ref, ab_ref,
                  pq_ref, gf_ref, ga_ref, q_t_ref, k_ref, v_t_ref):
    x = x_ref[...]
    tm = x.shape[0]
    mu = jnp.mean(x, axis=-1, keepdims=True)
    xc = x - mu
    var = jnp.mean(xc * xc, axis=-1, keepdims=True)
    h = xc * lax.rsqrt(var + NORM_EPS) * (1.0 + scale_ref[0]) + shift_ref[0]
    hb = h.astype(BF16)

    def proj(lo, hi):
        return jnp.dot(hb, w_in_ref[:, lo:hi], preferred_element_type=F32)

    ub = proj(COL_U, COL_ZF).astype(BF16)
    c = FOURIER_GROUP_DIM
    gate_f = _silu(proj(COL_ZF, COL_ZA)).astype(BF16)
    for g in range(FOURIER_GROUPS):
        pq = jnp.dot(ub[:, g * c:(g + 1) * c], ab_ref[g], preferred_element_type=F32)
        pq_ref[0, 0, g] = pq[:, :c].astype(BF16)
        pq_ref[0, 1, g] = pq[:, c:].astype(BF16)
        gf_ref[0, g] = gate_f[:, g * c:(g + 1) * c]

    ga_ref[...] = _silu(proj(COL_ZA, COL_CQ)).astype(BF16)

    cq = proj(COL_CQ, COL_CKV)
    cqn = (cq * lax.rsqrt(jnp.mean(cq * cq, axis=-1, keepdims=True) + NORM_EPS)
           * qn_ref[...]).astype(BF16)
    q_t = _nt_dot(wq_t_ref[...], cqn) * Q_SCALE
    cos_t = cos_t_ref[0]
    sin_t = sin_t_ref[0]
    n_nope = N_HEADS * QK_NOPE_DIM
    n_half = N_HEADS * HALF_ROPE
    zeros_q = jnp.zeros((QK_PAD_DIM - QK_HEAD_DIM, tm), BF16)
    for hd in range(N_HEADS):
        q_t_ref[0, hd, 0:QK_NOPE_DIM, :] = (
            q_t[hd * QK_NOPE_DIM:(hd + 1) * QK_NOPE_DIM].astype(BF16))
        x1 = q_t[n_nope + hd * HALF_ROPE:n_nope + (hd + 1) * HALF_ROPE]
        x2 = q_t[n_nope + n_half + hd * HALF_ROPE:n_nope + n_half + (hd + 1) * HALF_ROPE]
        q_t_ref[0, hd, QK_NOPE_DIM:QK_NOPE_DIM + HALF_ROPE, :] = (
            x1 * cos_t - x2 * sin_t).astype(BF16)
        q_t_ref[0, hd, QK_NOPE_DIM + HALF_ROPE:QK_HEAD_DIM, :] = (
            x2 * cos_t + x1 * sin_t).astype(BF16)
        q_t_ref[0, hd, QK_HEAD_DIM:QK_PAD_DIM, :] = zeros_q

    ckv_kr = proj(COL_CKV, W_IN_COLS)
    ckv = ckv_kr[:, :KV_LORA_RANK]
    ckvn = (ckv * lax.rsqrt(jnp.mean(ckv * ckv, axis=-1, keepdims=True) + NORM_EPS)
            * kvn_ref[...]).astype(BF16)
    k_nope = jnp.dot(ckvn, wk_ref[...], preferred_element_type=F32)
    v_t = _nt_dot(wv_t_ref[...], ckvn)
    kr = ckv_kr[:, KV_LORA_RANK:KV_LORA_RANK + QK_ROPE_DIM]
    kr_swapped = jnp.concatenate([kr[:, HALF_ROPE:], kr[:, :HALF_ROPE]], axis=1)
    kr_rot = (kr * cos2_ref[0] + kr_swapped * sin2_ref[0]).astype(BF16)
    kr_pad = jnp.concatenate(
        [kr_rot, jnp.zeros((tm, QK_PAD_DIM - QK_HEAD_DIM), BF16)], axis=1)
    for hd in range(N_HEADS):
        k_ref[0, hd, :, 0:QK_NOPE_DIM] = (
            k_nope[:, hd * QK_NOPE_DIM:(hd + 1) * QK_NOPE_DIM].astype(BF16))
        k_ref[0, hd, :, QK_NOPE_DIM:QK_PAD_DIM] = kr_pad
        v_t_ref[0, hd] = v_t[hd * V_HEAD_DIM:(hd + 1) * V_HEAD_DIM].astype(BF16)


def _resident(shape):
    zeros = (0,) * len(shape)
    return pl.BlockSpec(shape, lambda *_: zeros, pipeline_mode=pl.Buffered(1))


def _token_stage(x2d, scale, shift, tables, w):
    tm = 256
    nt = SEQ // tm
    cos_t, sin_t, cos2, sin2 = tables
    row = lambda i: (i, 0)
    per_batch = lambda i: (i // nt, 0, 0)
    bh = lambda i: (i // nt, 0, 0, i % nt)
    return pl.pallas_call(
        _token_kernel,
        out_shape=(
            jax.ShapeDtypeStruct((BATCH, 2, FOURIER_GROUPS, SEQ, FOURIER_GROUP_DIM), BF16),
            jax.ShapeDtypeStruct((BATCH, FOURIER_GROUPS, SEQ, FOURIER_GROUP_DIM), BF16),
            jax.ShapeDtypeStruct((TOKENS, D_ATTN), BF16),
            jax.ShapeDtypeStruct((BATCH, N_HEADS, QK_PAD_DIM, SEQ), BF16),
            jax.ShapeDtypeStruct((BATCH, N_HEADS, SEQ, QK_PAD_DIM), BF16),
            jax.ShapeDtypeStruct((BATCH, N_HEADS, V_HEAD_DIM, SEQ), BF16),
        ),
        grid=(TOKENS // tm,),
        in_specs=[
            pl.BlockSpec((tm, D_MODEL), row),
            pl.BlockSpec((1, 1, D_MODEL), per_batch),
            pl.BlockSpec((1, 1, D_MODEL), per_batch),
            pl.BlockSpec((1, HALF_ROPE, tm), lambda i: (i // nt, 0, i % nt)),
            pl.BlockSpec((1, HALF_ROPE, tm), lambda i: (i // nt, 0, i % nt)),
            pl.BlockSpec((1, tm, QK_ROPE_DIM), lambda i: (i // nt, i % nt, 0)),
            pl.BlockSpec((1, tm, QK_ROPE_DIM), lambda i: (i // nt, i % nt, 0)),
            _resident((D_MODEL, W_IN_COLS)),
            _resident((1, Q_LORA_RANK)),
            _resident((1, KV_LORA_RANK)),
            _resident((N_HEADS * QK_HEAD_DIM, Q_LORA_RANK)),
            _resident((KV_LORA_RANK, N_HEADS * QK_NOPE_DIM)),
            _resident((N_HEADS * V_HEAD_DIM, KV_LORA_RANK)),
            _resident((FOURIER_GROUPS, FOURIER_GROUP_DIM, 2 * FOURIER_GROUP_DIM)),
        ],
        out_specs=(
            pl.BlockSpec((1, 2, FOURIER_GROUPS, tm, FOURIER_GROUP_DIM),
                         lambda i: (i // nt, 0, 0, i % nt, 0)),
            pl.BlockSpec((1, FOURIER_GROUPS, tm, FOURIER_GROUP_DIM),
                         lambda i: (i // nt, 0, i % nt, 0)),
            pl.BlockSpec((tm, D_ATTN), row),
            pl.BlockSpec((1, N_HEADS, QK_PAD_DIM, tm), bh),
            pl.BlockSpec((1, N_HEADS, tm, QK_PAD_DIM), lambda i: (i // nt, 0, i % nt, 0)),
            pl.BlockSpec((1, N_HEADS, V_HEAD_DIM, tm), bh),
        ),
        compiler_params=pltpu.CompilerParams(
            dimension_semantics=("arbitrary",), vmem_limit_bytes=VMEM_LIMIT),
        name="token_stage",
    )(x2d, scale, shift, cos_t, sin_t, cos2, sin2,
      w["w_in"], w["q_norm"], w["kv_norm"], w["wq_t"], w["wk"], w["wv_t"], w["ab"])


FFT_R = 64
FFT_LANES = FFT_R * FOURIER_GROUP_DIM
FFT_CHUNK = 1024


def _fft_constants():
    idx = np.arange(FFT_R)
    ang = 2.0 * np.pi * ((idx[:, None] * idx[None, :]) % FFT_R) / FFT_R
    c, s = np.cos(ang), np.sin(ang)
    norm = 1.0 / math.sqrt(FFT_R)
    m1 = np.block([[c, -s], [-s, -c]]) * norm
    m3 = np.concatenate([c, s], axis=1) * norm
    tw_ang = 2.0 * np.pi * (idx[:, None] * idx[None, :]) / SEQ
    tw_cos = np.repeat(np.cos(tw_ang), FOURIER_GROUP_DIM, axis=1)
    tw_sin = np.repeat(np.sin(tw_ang), FOURIER_GROUP_DIM, axis=1)
    return (jnp.asarray(m1, F32).astype(BF16), jnp.asarray(m3, F32).astype(BF16),
            jnp.asarray(tw_cos, F32), jnp.asarray(tw_sin, F32))


def _fft_stage1_kernel(m1_ref, twc_ref, tws_ref, pq_ref, o_ref):
    r = FFT_R
    for ci in range(FFT_LANES // FFT_CHUNK):
        lanes = slice(ci * FFT_CHUNK, (ci + 1) * FFT_CHUNK)
        z = jnp.concatenate([pq_ref[0, 0, 0, :, lanes], pq_ref[0, 1, 0, :, lanes]], axis=0)
        a = jnp.dot(m1_ref[...], z, preferred_element_type=F32)
        a_r, a_i = a[:r], a[r:]
        tc, ts = twc_ref[:, lanes], tws_ref[:, lanes]
        b_r = (a_r * tc + a_i * ts).astype(BF16)
        b_i = (a_i * tc - a_r * ts).astype(BF16)
        for j in range(FFT_CHUNK // FOURIER_GROUP_DIM):
            s2 = ci * (FFT_CHUNK // FOURIER_GROUP_DIM) + j
            cols = slice(j * FOURIER_GROUP_DIM, (j + 1) * FOURIER_GROUP_DIM)
            o_ref[0, 0, 0, s2 * r:(s2 + 1) * r, :] = b_r[:, cols]
            o_ref[0, 0, 1, s2 * r:(s2 + 1) * r, :] = b_i[:, cols]


def _fft_stage2_kernel(m3_ref, b_ref, g_ref, o_ref):
    for ci in range(FFT_LANES // FFT_CHUNK):
        lanes = slice(ci * FFT_CHUNK, (ci + 1) * FFT_CHUNK)
        b = jnp.concatenate([b_ref[0, 0, 0, :, lanes], b_ref[0, 0, 1, :, lanes]], axis=0)
        y = jnp.dot(m3_ref[...], b, preferred_element_type=F32)
        o_ref[0, 0, :, lanes] = (y * g_ref[0, 0, :, lanes].astype(F32)).astype(BF16)


def _seq_fft(pq, gate_f, consts):
    m1, m3, tw_cos, tw_sin = consts
    g, r = FOURIER_GROUPS, FFT_R
    params = pltpu.CompilerParams(
        dimension_semantics=("arbitrary", "arbitrary"), vmem_limit_bytes=VMEM_LIMIT)
    mid = pl.pallas_call(
        _fft_stage1_kernel,
        out_shape=jax.ShapeDtypeStruct((BATCH, g, 2, SEQ, FOURIER_GROUP_DIM), BF16),
        grid=(BATCH, g),
        in_specs=[
            _resident((2 * r, 2 * r)),
            _resident((r, FFT_LANES)),
            _resident((r, FFT_LANES)),
            pl.BlockSpec((1, 2, 1, r, FFT_LANES), lambda b, gi: (b, 0, gi, 0, 0)),
        ],
        out_specs=pl.BlockSpec((1, 1, 2, SEQ, FOURIER_GROUP_DIM), lambda b, gi: (b, gi, 0, 0, 0)),
        compiler_params=params,
        name="fft_stage1",
    )(m1, tw_cos, tw_sin, pq.reshape(BATCH, 2, g, r, FFT_LANES))
    y = pl.pallas_call(
        _fft_stage2_kernel,
        out_shape=jax.ShapeDtypeStruct((BATCH, g, r, FFT_LANES), BF16),
        grid=(BATCH, g),
        in_specs=[
            _resident((r, 2 * r)),
            pl.BlockSpec((1, 1, 2, r, FFT_LANES), lambda b, gi: (b, gi, 0, 0, 0)),
            pl.BlockSpec((1, 1, r, FFT_LANES), lambda b, gi: (b, gi, 0, 0)),
        ],
        out_specs=pl.BlockSpec((1, 1, r, FFT_LANES), lambda b, gi: (b, gi, 0, 0)),
        compiler_params=params,
        name="fft_stage2",
    )(m3, mid.reshape(BATCH, g, 2, r, FFT_LANES), gate_f.reshape(BATCH, g, r, FFT_LANES))
    return y.reshape(BATCH, g, SEQ, FOURIER_GROUP_DIM)


def _attn_kernel(q_t_ref, k_ref, v_t_ref, g_ref, o_ref, s_scr, p_scr, acc_scr, *, tk):
    tq = q_t_ref.shape[3]
    n_chunks = SEQ // tk

    def scores(c, slot):
        off = pl.multiple_of(c * tk, tk)
        s = jnp.dot(k_ref[0, 0, pl.ds(off, tk), :], q_t_ref[0, 0],
                    preferred_element_type=F32)
        s_scr[slot] = s
        return jnp.max(s, axis=0, keepdims=True)

    def softmax(slot, m_chunk, m, l):
        m_new = jnp.maximum(m, m_chunk)
        alpha = jnp.exp2(m - m_new)
        p = jnp.exp2(s_scr[slot] - m_new)
        p_scr[slot] = p.astype(BF16)
        return m_new, alpha * l + jnp.sum(p, axis=0, keepdims=True), alpha

    def values(c, slot, alpha):
        off = pl.multiple_of(c * tk, tk)
        acc_scr[...] = alpha * acc_scr[...] + jnp.dot(
            v_t_ref[0, 0, :, pl.ds(off, tk)], p_scr[slot], preferred_element_type=F32)

    def step(c, slot, carry):
        m, l, alpha_prev, m_chunk = carry
        m_chunk_next = scores(c + 1, 1 - slot)
        m, l, alpha = softmax(slot, m_chunk, m, l)
        values(c - 1, 1 - slot, alpha_prev)
        return m, l, alpha, m_chunk_next

    acc_scr[...] = jnp.zeros_like(acc_scr)
    m_chunk = scores(0, 0)
    m_chunk_next = scores(1, 1)
    m, l, alpha = softmax(0, m_chunk, jnp.full((1, tq), -jnp.inf, F32), jnp.zeros((1, tq), F32))
    carry = (m, l, alpha, m_chunk_next)

    def pair(j, carry):
        c = 1 + 2 * j
        return step(c + 1, 0, step(c, 1, carry))

    for j in range((n_chunks - 2) // 2):
        carry = pair(j, carry)
    m, l, alpha_prev, m_chunk = carry
    last = n_chunks - 1
    m, l, alpha = softmax(1, m_chunk, m, l)
    values(last - 1, 0, alpha_prev)
    values(last, 1, alpha)
    o = (acc_scr[...] * (1.0 / l)).T
    o_ref[...] = (o * g_ref[...].astype(F32)).astype(BF16)


def _attention(q_t, k, v_t, gate_a):
    tq, tk = 512, 512
    nq = SEQ // tq
    assert (SEQ // tk) % 2 == 0
    return pl.pallas_call(
        functools.partial(_attn_kernel, tk=tk),
        scratch_shapes=[
            pltpu.VMEM((2, tk, tq), F32),
            pltpu.VMEM((2, tk, tq), BF16),
            pltpu.VMEM((V_HEAD_DIM, tq), F32),
        ],
        out_shape=jax.ShapeDtypeStruct((TOKENS, D_ATTN), BF16),
        grid=(BATCH, N_HEADS, nq),
        in_specs=[
            pl.BlockSpec((1, 1, QK_PAD_DIM, tq), lambda b, h, i: (b, h, 0, i)),
            pl.BlockSpec((1, 1, SEQ, QK_PAD_DIM), lambda b, h, i: (b, h, 0, 0)),
            pl.BlockSpec((1, 1, V_HEAD_DIM, SEQ), lambda b, h, i: (b, h, 0, 0)),
            pl.BlockSpec((tq, V_HEAD_DIM), lambda b, h, i: (b * nq + i, h)),
        ],
        out_specs=pl.BlockSpec((tq, V_HEAD_DIM), lambda b, h, i: (b * nq + i, h)),
        compiler_params=pltpu.CompilerParams(
            dimension_semantics=("arbitrary", "arbitrary", "arbitrary"),
            vmem_limit_bytes=VMEM_LIMIT),
        name="attention",
    )(q_t, k, v_t, gate_a)


OUT_SUB_ROWS = 256


def _out_kernel(yf_ref, ya_ref, x_ref, gate_ref, w_ref, g_ref, b_ref, o_ref):
    for rows in (slice(i * OUT_SUB_ROWS, (i + 1) * OUT_SUB_ROWS)
                 for i in range(x_ref.shape[0] // OUT_SUB_ROWS)):
        y_f = jnp.concatenate([yf_ref[0, g, rows, :] for g in range(FOURIER_GROUPS)], axis=1)
        y = (jnp.dot(y_f, w_ref[:D_FOURIER, :], preferred_element_type=F32)
             + jnp.dot(ya_ref[rows, :], w_ref[D_FOURIER:, :], preferred_element_type=F32))
        r = DEEPNORM_ALPHA * x_ref[rows, :] + gate_ref[0] * y
        mu = jnp.mean(r, axis=-1, keepdims=True)
        rc = r - mu
        var = jnp.mean(rc * rc, axis=-1, keepdims=True)
        o_ref[rows, :] = rc * lax.rsqrt(var + NORM_EPS) * g_ref[...] + b_ref[...]


def _output_stage(y_f, y_a, x2d, gate, w_out_bf16, ln_g, ln_b):
    tm = 512
    nt = SEQ // tm
    row = lambda i: (i, 0)
    return pl.pallas_call(
        _out_kernel,
        out_shape=jax.ShapeDtypeStruct((TOKENS, D_MODEL), F32),
        grid=(TOKENS // tm,),
        in_specs=[
            pl.BlockSpec((1, FOURIER_GROUPS, tm, FOURIER_GROUP_DIM),
                         lambda i: (i // nt, 0, i % nt, 0)),
            pl.BlockSpec((tm, D_ATTN), row),
            pl.BlockSpec((tm, D_MODEL), row),
            pl.BlockSpec((1, 1, D_MODEL), lambda i: (i // nt, 0, 0)),
            _resident((D_MODEL, D_MODEL)),
            _resident((1, D_MODEL)),
            _resident((1, D_MODEL)),
        ],
        out_specs=pl.BlockSpec((tm, D_MODEL), row),
        compiler_params=pltpu.CompilerParams(
            dimension_semantics=("arbitrary",), vmem_limit_bytes=VMEM_LIMIT),
        name="output_stage",
    )(y_f, y_a, x2d, gate, w_out_bf16, ln_g, ln_b)


def _layer_weights(w_in_l, q_norm_l, w_q_b_l, kv_norm_l, w_kv_b_l, w_fmix_l):
    u_f, z_f, cq, ckv, k_r, z_a = jnp.split(
        w_in_l.astype(BF16), [1024, 2048, 2560, 2816, 2880], axis=1)
    pad = jnp.zeros((D_MODEL, W_IN_COLS - w_in_l.shape[1]), BF16)
    w_in = jnp.concatenate([u_f, z_f, z_a, cq, ckv, k_r, pad], axis=1)
    wq = w_q_b_l.reshape(Q_LORA_RANK, N_HEADS, QK_HEAD_DIM)
    wq = jnp.concatenate([
        wq[:, :, :QK_NOPE_DIM].reshape(Q_LORA_RANK, -1),
        wq[:, :, QK_NOPE_DIM:QK_NOPE_DIM + HALF_ROPE].reshape(Q_LORA_RANK, -1),
        wq[:, :, QK_NOPE_DIM + HALF_ROPE:].reshape(Q_LORA_RANK, -1)], axis=1)
    wkv = w_kv_b_l.reshape(KV_LORA_RANK, N_HEADS, QK_NOPE_DIM + V_HEAD_DIM)
    wk = wkv[:, :, :QK_NOPE_DIM].reshape(KV_LORA_RANK, -1)
    wv = wkv[:, :, QK_NOPE_DIM:].reshape(KV_LORA_RANK, -1)
    return {
        "w_in": w_in,
        "q_norm": q_norm_l.reshape(1, Q_LORA_RANK),
        "kv_norm": kv_norm_l.reshape(1, KV_LORA_RANK),
        "wq_t": wq.T.astype(BF16),
        "wk": wk.astype(BF16),
        "wv_t": wv.T.astype(BF16),
        "ab": _fold_fourier_weights(w_fmix_l),
    }


def kernel(x, c, positions, w_ada, b_ada, w_in, q_norm, w_q_b, kv_norm, w_kv_b, w_fmix, w_out,
           ln_g, ln_b):
    assert x.shape == (BATCH, SEQ, D_MODEL) and w_ada.shape[0] == DEPTH
    mod = _ada_modulation(c, w_ada, b_ada)
    tables = _rope_tables(positions)
    fft_consts = _fft_constants()
    x2d = x.reshape(TOKENS, D_MODEL)
    for l in range(DEPTH):
        shift, scale, gate = (
            mod[l, :BATCH, i * D_MODEL:(i + 1) * D_MODEL].reshape(BATCH, 1, D_MODEL)
            for i in range(3))
        w = _layer_weights(w_in[l], q_norm[l], w_q_b[l], kv_norm[l], w_kv_b[l], w_fmix[l])
        pq, gate_f, gate_a, q_t, k, v_t = _token_stage(x2d, scale, shift, tables, w)
        y_f = _seq_fft(pq, gate_f, fft_consts)
        y_a = _attention(q_t, k, v_t, gate_a)
        x2d = _output_stage(y_f, y_a, x2d, gate, w_out[l].astype(BF16),
                            ln_g[l].reshape(1, D_MODEL), ln_b[l].reshape(1, D_MODEL))
    return x2d.reshape(BATCH, SEQ, D_MODEL)
```

```python
import functools
import math

import numpy as np
import jax
import jax.numpy as jnp
from jax import lax
from jax.experimental import pallas as pl
from jax.experimental.pallas import tpu as pltpu

D_MODEL = 2048
BATCH = 4
SEQ = 4096
DEPTH = 2
D_FOURIER = D_MODEL // 2
FOURIER_GROUPS = 8
FOURIER_GROUP_DIM = D_FOURIER // FOURIER_GROUPS
V_HEAD_DIM = 128
D_ATTN = D_MODEL // 2
N_HEADS = D_ATTN // V_HEAD_DIM
QK_NOPE_DIM = 128
QK_ROPE_DIM = 64
QK_HEAD_DIM = QK_NOPE_DIM + QK_ROPE_DIM
Q_LORA_RANK = D_MODEL // 4
KV_LORA_RANK = D_MODEL // 8
ROPE_THETA = 10000.0
NORM_EPS = 1e-6
DEEPNORM_ALPHA = (2 * DEPTH) ** 0.25

TOKENS = BATCH * SEQ
HALF_ROPE = QK_ROPE_DIM // 2
QK_PAD_DIM = 256
W_IN_COLS = 3968
COL_U, COL_ZF, COL_ZA, COL_CQ, COL_CKV = 0, 1024, 2048, 3072, 3584

F32 = jnp.float32
BF16 = jnp.bfloat16
VMEM_LIMIT = 52 * 1024 * 1024

Q_SCALE = math.log2(math.e) / math.sqrt(QK_HEAD_DIM)


def _silu(v):
    return v * (1.0 / (1.0 + jnp.exp(-v)))


def _nt_dot(a, b):
    return lax.dot_general(a, b, (((1,), (1,)), ((), ())), preferred_element_type=F32)


def _ada_kernel(c_ref, w_ref, b_ref, o_ref):
    c_act = _silu(c_ref[...]).astype(BF16)
    w = w_ref[0].astype(BF16)
    o_ref[0] = jnp.dot(c_act, w, preferred_element_type=F32) + b_ref[0]


def _ada_modulation(c, w_ada, b_ada):
    tn = 768
    c_pad = jnp.pad(c, ((0, 8 - BATCH), (0, 0)))
    b3 = b_ada.reshape(DEPTH, 1, 3 * D_MODEL)
    return pl.pallas_call(
        _ada_kernel,
        out_shape=jax.ShapeDtypeStruct((DEPTH, 8, 3 * D_MODEL), F32),
        grid=(DEPTH, 3 * D_MODEL // tn),
        in_specs=[
            pl.BlockSpec((8, D_MODEL), lambda l, j: (0, 0)),
            pl.BlockSpec((1, D_MODEL, tn), lambda l, j: (l, 0, j)),
            pl.BlockSpec((1, 1, tn), lambda l, j: (l, 0, j)),
        ],
        out_specs=pl.BlockSpec((1, 8, tn), lambda l, j: (l, 0, j)),
        compiler_params=pltpu.CompilerParams(
            dimension_semantics=("arbitrary", "arbitrary"), vmem_limit_bytes=VMEM_LIMIT),
        name="ada_modulation",
    )(c_pad, w_ada, b3)


def _rope_kernel(pos_row_ref, pos_col_ref, invf_col_ref, invf_row_ref,
                 cos_t_ref, sin_t_ref, cos2_ref, sin2_ref):
    ang_t = invf_col_ref[...] * pos_row_ref[0].astype(F32)
    cos_t_ref[0] = jnp.cos(ang_t)
    sin_t_ref[0] = jnp.sin(ang_t)
    ang2 = pos_col_ref[0].astype(F32) * invf_row_ref[...]
    lane = lax.broadcasted_iota(jnp.int32, ang2.shape, 1)
    cos2_ref[0] = jnp.cos(ang2)
    sin2_ref[0] = jnp.where(lane < HALF_ROPE, -jnp.sin(ang2), jnp.sin(ang2))


def _rope_tables(positions):
    ts = 1024
    inv_freq = ROPE_THETA ** (-jnp.arange(0, QK_ROPE_DIM, 2, dtype=F32) / QK_ROPE_DIM)
    invf_col = inv_freq.reshape(HALF_ROPE, 1)
    invf_row = jnp.concatenate([inv_freq, inv_freq]).reshape(1, QK_ROPE_DIM)
    pos_row = positions.reshape(BATCH, 1, SEQ)
    pos_col = positions.reshape(BATCH, SEQ, 1)
    return pl.pallas_call(
        _rope_kernel,
        out_shape=(
            jax.ShapeDtypeStruct((BATCH, HALF_ROPE, SEQ), F32),
            jax.ShapeDtypeStruct((BATCH, HALF_ROPE, SEQ), F32),
            jax.ShapeDtypeStruct((BATCH, SEQ, QK_ROPE_DIM), F32),
            jax.ShapeDtypeStruct((BATCH, SEQ, QK_ROPE_DIM), F32),
        ),
        grid=(BATCH, SEQ // ts),
        in_specs=[
            pl.BlockSpec((1, 1, ts), lambda b, s: (b, 0, s)),
            pl.BlockSpec((1, ts, 1), lambda b, s: (b, s, 0)),
            pl.BlockSpec((HALF_ROPE, 1), lambda b, s: (0, 0)),
            pl.BlockSpec((1, QK_ROPE_DIM), lambda b, s: (0, 0)),
        ],
        out_specs=(
            pl.BlockSpec((1, HALF_ROPE, ts), lambda b, s: (b, 0, s)),
            pl.BlockSpec((1, HALF_ROPE, ts), lambda b, s: (b, 0, s)),
            pl.BlockSpec((1, ts, QK_ROPE_DIM), lambda b, s: (b, s, 0)),
            pl.BlockSpec((1, ts, QK_ROPE_DIM), lambda b, s: (b, s, 0)),
        ),
        compiler_params=pltpu.CompilerParams(
            dimension_semantics=("arbitrary", "arbitrary"), vmem_limit_bytes=VMEM_LIMIT),
        name="rope_tables",
    )(pos_row, pos_col, invf_col, invf_row)


def _channel_dft_matrix():
    idx = np.arange(FOURIER_GROUP_DIM)
    ang = 2.0 * np.pi * ((idx[:, None] * idx[None, :]) % FOURIER_GROUP_DIM) / FOURIER_GROUP_DIM
    scale = 1.0 / math.sqrt(FOURIER_GROUP_DIM)
    return np.concatenate([np.cos(ang), np.sin(ang)], axis=1).astype(np.float32) * np.float32(scale)


def _fold_kernel(cs_ref, w_ref, o_ref):
    w = w_ref[0]
    dims = (((1,), (0,)), ((), ()))
    a = lax.dot_general(cs_ref[:, :FOURIER_GROUP_DIM], w, dims,
                        precision=lax.Precision.HIGHEST, preferred_element_type=F32)
    b = lax.dot_general(cs_ref[:, FOURIER_GROUP_DIM:], w, dims,
                        precision=lax.Precision.HIGHEST, preferred_element_type=F32)
    o_ref[0] = jnp.concatenate([a, b], axis=1).astype(BF16)


def _fold_fourier_weights(w_fmix_l):
    g, c = FOURIER_GROUPS, FOURIER_GROUP_DIM
    return pl.pallas_call(
        _fold_kernel,
        out_shape=jax.ShapeDtypeStruct((g, c, 2 * c), BF16),
        grid=(g,),
        in_specs=[
            pl.BlockSpec((c, 2 * c), lambda i: (0, 0)),
            pl.BlockSpec((1, c, c), lambda i: (i, 0, 0)),
        ],
        out_specs=pl.BlockSpec((1, c, 2 * c), lambda i: (i, 0, 0)),
        compiler_params=pltpu.CompilerParams(dimension_semantics=("arbitrary",)),
        name="fold_fourier_weights",
    )(jnp.asarray(_channel_dft_matrix()), w_fmix_l)


def _token_kernel(x_ref, scale_ref, shift_ref, cos_t_ref, sin_t_ref, cos2_ref, sin2_ref,
                  w_in_ref, qn_ref, kvn_ref, wq_t_ref, wk_ref, wv_t_ref, ab_ref,
                  pq_ref, gf_ref, ga_ref, q_t_ref, k_ref, v_t_ref):
    x = x_ref[...]
    tm = x.shape[0]
    mu = jnp.mean(x, axis=-1, keepdims=True)
    xc = x - mu
    var = jnp.mean(xc * xc, axis=-1, keepdims=True)
    h = xc * lax.rsqrt(var + NORM_EPS) * (1.0 + scale_ref[0]) + shift_ref[0]
    hb = h.astype(BF16)

    def proj(lo, hi):
        return jnp.dot(hb, w_in_ref[:, lo:hi], preferred_element_type=F32)

    ub = proj(COL_U, COL_ZF).astype(BF16)
    c = FOURIER_GROUP_DIM
    gate_f = _silu(proj(COL_ZF, COL_ZA)).astype(BF16)
    for g in range(FOURIER_GROUPS):
        pq = jnp.dot(ub[:, g * c:(g + 1) * c], ab_ref[g], preferred_element_type=F32)
        pq_ref[0, 0, g] = pq[:, :c].astype(BF16)
        pq_ref[0, 1, g] = pq[:, c:].astype(BF16)
        gf_ref[0, g] = gate_f[:, g * c:(g + 1) * c]

    ga_ref[...] = _silu(proj(COL_ZA, COL_CQ)).astype(BF16)

    cq = proj(COL_CQ, COL_CKV)
    cqn = (cq * lax.rsqrt(jnp.mean(cq * cq, axis=-1, keepdims=True) + NORM_EPS)
           * qn_ref[...]).astype(BF16)
    q_t = _nt_dot(wq_t_ref[...], cqn) * Q_SCALE
    cos_t = cos_t_ref[0]
    sin_t = sin_t_ref[0]
    n_nope = N_HEADS * QK_NOPE_DIM
    n_half = N_HEADS * HALF_ROPE
    zeros_q = jnp.zeros((QK_PAD_DIM - QK_HEAD_DIM, tm), BF16)
    for hd in range(N_HEADS):
        q_t_ref[0, hd, 0:QK_NOPE_DIM, :] = (
            q_t[hd * QK_NOPE_DIM:(hd + 1) * QK_NOPE_DIM].astype(BF16))
        x1 = q_t[n_nope + hd * HALF_ROPE:n_nope + (hd + 1) * HALF_ROPE]
        x2 = q_t[n_nope + n_half + hd * HALF_ROPE:n_nope + n_half + (hd + 1) * HALF_ROPE]
        q_t_ref[0, hd, QK_NOPE_DIM:QK_NOPE_DIM + HALF_ROPE, :] = (
            x1 * cos_t - x2 * sin_t).astype(BF16)
        q_t_ref[0, hd, QK_NOPE_DIM + HALF_ROPE:QK_HEAD_DIM, :] = (
            x2 * cos_t + x1 * sin_t).astype(BF16)
        q_t_ref[0, hd, QK_HEAD_DIM:QK_PAD_DIM, :] = zeros_q

    ckv_kr = proj(COL_CKV, W_IN_COLS)
    ckv = ckv_kr[:, :KV_LORA_RANK]
    ckvn = (ckv * lax.rsqrt(jnp.mean(ckv * ckv, axis=-1, keepdims=True) + NORM_EPS)
            * kvn_ref[...]).astype(BF16)
    k_nope = jnp.dot(ckvn, wk_ref[...], preferred_element_type=F32)
    v_t = _nt_dot(wv_t_ref[...], ckvn)
    kr = ckv_kr[:, KV_LORA_RANK:KV_LORA_RANK + QK_ROPE_DIM]
    kr_swapped = jnp.concatenate([kr[:, HALF_ROPE:], kr[:, :HALF_ROPE]], axis=1)
    kr_rot = (kr * cos2_ref[0] + kr_swapped * sin2_ref[0]).astype(BF16)
    kr_pad = jnp.concatenate(
        [kr_rot, jnp.zeros((tm, QK_PAD_DIM - QK_HEAD_DIM), BF16)], axis=1)
    for hd in range(N_HEADS):
        k_ref[0, hd, :, 0:QK_NOPE_DIM] = (
            k_nope[:, hd * QK_NOPE_DIM:(hd + 1) * QK_NOPE_DIM].astype(BF16))
        k_ref[0, hd, :, QK_NOPE_DIM:QK_PAD_DIM] = kr_pad
        v_t_ref[0, hd] = v_t[hd * V_HEAD_DIM:(hd + 1) * V_HEAD_DIM].astype(BF16)


def _resident(shape):
    zeros = (0,) * len(shape)
    return pl.BlockSpec(shape, lambda *_: zeros, pipeline_mode=pl.Buffered(1))


def _token_stage(x2d, scale, shift, tables, w):
    tm = 256
    nt = SEQ // tm
    cos_t, sin_t, cos2, sin2 = tables
    row = lambda i: (i, 0)
    per_batch = lambda i: (i // nt, 0, 0)
    bh = lambda i: (i // nt, 0, 0, i % nt)
    return pl.pallas_call(
        _token_kernel,
        out_shape=(
            jax.ShapeDtypeStruct((BATCH, 2, FOURIER_GROUPS, SEQ, FOURIER_GROUP_DIM), BF16),
            jax.ShapeDtypeStruct((BATCH, FOURIER_GROUPS, SEQ, FOURIER_GROUP_DIM), BF16),
            jax.ShapeDtypeStruct((TOKENS, D_ATTN), BF16),
            jax.ShapeDtypeStruct((BATCH, N_HEADS, QK_PAD_DIM, SEQ), BF16),
            jax.ShapeDtypeStruct((BATCH, N_HEADS, SEQ, QK_PAD_DIM), BF16),
            jax.ShapeDtypeStruct((BATCH, N_HEADS, V_HEAD_DIM, SEQ), BF16),
        ),
        grid=(TOKENS // tm,),
        in_specs=[
            pl.BlockSpec((tm, D_MODEL), row),
            pl.BlockSpec((1, 1, D_MODEL), per_batch),
            pl.BlockSpec((1, 1, D_MODEL), per_batch),
            pl.BlockSpec((1, HALF_ROPE, tm), lambda i: (i // nt, 0, i % nt)),
            pl.BlockSpec((1, HALF_ROPE, tm), lambda i: (i // nt, 0, i % nt)),
            pl.BlockSpec((1, tm, QK_ROPE_DIM), lambda i: (i // nt, i % nt, 0)),
            pl.BlockSpec((1, tm, QK_ROPE_DIM), lambda i: (i // nt, i % nt, 0)),
            _resident((D_MODEL, W_IN_COLS)),
            _resident((1, Q_LORA_RANK)),
            _resident((1, KV_LORA_RANK)),
            _resident((N_HEADS * QK_HEAD_DIM, Q_LORA_RANK)),
            _resident((KV_LORA_RANK, N_HEADS * QK_NOPE_DIM)),
            _resident((N_HEADS * V_HEAD_DIM, KV_LORA_RANK)),
            _resident((FOURIER_GROUPS, FOURIER_GROUP_DIM, 2 * FOURIER_GROUP_DIM)),
        ],
        out_specs=(
            pl.BlockSpec((1, 2, FOURIER_GROUPS, tm, FOURIER_GROUP_DIM),
                         lambda i: (i // nt, 0, 0, i % nt, 0)),
            pl.BlockSpec((1, FOURIER_GROUPS, tm, FOURIER_GROUP_DIM),
                         lambda i: (i // nt, 0, i % nt, 0)),
            pl.BlockSpec((tm, D_ATTN), row),
            pl.BlockSpec((1, N_HEADS, QK_PAD_DIM, tm), bh),
            pl.BlockSpec((1, N_HEADS, tm, QK_PAD_DIM), lambda i: (i // nt, 0, i % nt, 0)),
            pl.BlockSpec((1, N_HEADS, V_HEAD_DIM, tm), bh),
        ),
        compiler_params=pltpu.CompilerParams(
            dimension_semantics=("arbitrary",), vmem_limit_bytes=VMEM_LIMIT),
        name="token_stage",
    )(x2d, scale, shift, cos_t, sin_t, cos2, sin2,
      w["w_in"], w["q_norm"], w["kv_norm"], w["wq_t"], w["wk"], w["wv_t"], w["ab"])


DFT_LEVELS = 3
DFT_BLOCKS = 2 ** DFT_LEVELS
DFT_SUB = SEQ // DFT_BLOCKS
DFT_GROUPS_PER_STEP = 2
DFT_ROW_CHUNK = 32


def _dft_constants():
    tw = []
    for level in range(DFT_LEVELS):
        length = SEQ >> level
        ang = 2.0 * np.pi * np.arange(length // 2) / length
        for f in (np.cos, np.sin):
            col = jnp.asarray(f(ang).astype(np.float32)).reshape(-1, 1)
            tw.append(jnp.broadcast_to(col, (length // 2, FOURIER_GROUP_DIM)))
    idx = np.arange(DFT_SUB)
    ang = 2.0 * np.pi * ((idx[:, None] * idx[None, :]) % DFT_SUB) / DFT_SUB
    mat = np.concatenate([np.cos(ang), -np.sin(ang)], axis=1) / math.sqrt(SEQ)
    return tw, jnp.asarray(mat.astype(np.float32)).astype(BF16)


def _bit_reverse(j, bits):
    return int(format(j, "0{}b".format(bits))[::-1], 2)


def _dft_kernel(*refs):
    tw_refs = refs[:2 * DFT_LEVELS]
    m_ref, pq_ref, g_ref, o_ref, p_scr, w_scr = refs[2 * DFT_LEVELS:2 * DFT_LEVELS + 6]
    y_scr = refs[2 * DFT_LEVELS + 6:]
    gps, c, rc = DFT_GROUPS_PER_STEP, FOURIER_GROUP_DIM, DFT_ROW_CHUNK

    def load_input(g, row):
        rows = pl.ds(pl.multiple_of(row, rc), rc)
        return pq_ref[0, 0, g, rows, :].astype(F32), pq_ref[0, 1, g, rows, :].astype(F32)

    def load_scratch(g, row):
        rows = pl.ds(pl.multiple_of(row, rc), rc)
        return p_scr[rows, g * c:(g + 1) * c], w_scr[rows, g * c:(g + 1) * c]

    def store_scratch(g, row, p, w):
        rows = pl.ds(pl.multiple_of(row, rc), rc)
        p_scr[rows, g * c:(g + 1) * c] = p
        w_scr[rows, g * c:(g + 1) * c] = w

    for level in range(DFT_LEVELS):
        half = SEQ >> (level + 1)
        load = load_input if level == 0 else load_scratch
        twc_ref, tws_ref = tw_refs[2 * level], tw_refs[2 * level + 1]

        def body(i, carry, half=half, load=load, twc_ref=twc_ref, tws_ref=tws_ref, level=level):
            r = i * rc
            tw_rows = pl.ds(pl.multiple_of(r, rc), rc)
            cos, sin = twc_ref[tw_rows, :], tws_ref[tw_rows, :]
            for blk in range(2 ** level):
                base = blk * 2 * half
                for g in range(gps):
                    pa, wa = load(g, base + r)
                    pb, wb = load(g, base + half + r)
                    store_scratch(g, base + r, pa + pb, wa + wb)
                    dp, dw = pa - pb, wa - wb
                    store_scratch(g, base + half + r, dp * cos - dw * sin, dw * cos + dp * sin)
            return carry

        lax.fori_loop(0, half // rc, body, 0)

    for j in range(DFT_BLOCKS):
        rows = slice(j * DFT_SUB, (j + 1) * DFT_SUB)
        z = jnp.concatenate([p_scr[rows, :], w_scr[rows, :]], axis=0).astype(BF16)
        y = jnp.dot(m_ref[...], z, preferred_element_type=F32)
        out_rows = pl.ds(_bit_reverse(j, DFT_LEVELS), DFT_SUB, stride=DFT_BLOCKS)
        for g in range(gps):
            y_scr[g][out_rows, :] = y[:, g * c:(g + 1) * c]
    for g in range(gps):
        o_ref[0, g] = (y_scr[g][...] * g_ref[0, g].astype(F32)).astype(BF16)


def _seq_dft(pq, gate_f, consts):
    tw, mat = consts
    gps, c = DFT_GROUPS_PER_STEP, FOURIER_GROUP_DIM
    n_gsteps = FOURIER_GROUPS // gps
    return pl.pallas_call(
        _dft_kernel,
        out_shape=jax.ShapeDtypeStruct((BATCH, FOURIER_GROUPS, SEQ, c), BF16),
        grid=(BATCH, n_gsteps),
        in_specs=(
            [_resident(t.shape) for t in tw]
            + [_resident(mat.shape),
               pl.BlockSpec((1, 2, gps, SEQ, c), lambda b, gi: (b, 0, gi, 0, 0)),
               pl.BlockSpec((1, gps, SEQ, c), lambda b, gi: (b, gi, 0, 0))]),
        out_specs=pl.BlockSpec((1, gps, SEQ, c), lambda b, gi: (b, gi, 0, 0)),
        scratch_shapes=(
            [pltpu.VMEM((SEQ, gps * c), F32)] * 2 + [pltpu.VMEM((SEQ, c), F32)] * gps),
        compiler_params=pltpu.CompilerParams(
            dimension_semantics=("arbitrary", "arbitrary"), vmem_limit_bytes=VMEM_LIMIT),
        name="seq_dft",
    )(*tw, mat, pq, gate_f)


def _attn_kernel(q_t_ref, k_ref, v_t_ref, g_ref, o_ref, s_scr, p_scr, acc_scr, *, tk):
    tq = q_t_ref.shape[3]
    n_chunks = SEQ // tk

    def scores(c, slot):
        off = pl.multiple_of(c * tk, tk)
        s = jnp.dot(k_ref[0, 0, pl.ds(off, tk), :], q_t_ref[0, 0],
                    preferred_element_type=F32)
        s_scr[slot] = s
        return jnp.max(s, axis=0, keepdims=True)

    def softmax(slot, m_chunk, m, l):
        m_new = jnp.maximum(m, m_chunk)
        alpha = jnp.exp2(m - m_new)
        p = jnp.exp2(s_scr[slot] - m_new)
        p_scr[slot] = p.astype(BF16)
        return m_new, alpha * l + jnp.sum(p, axis=0, keepdims=True), alpha

    def values(c, slot, alpha):
        off = pl.multiple_of(c * tk, tk)
        acc_scr[...] = alpha * acc_scr[...] + jnp.dot(
            v_t_ref[0, 0, :, pl.ds(off, tk)], p_scr[slot], preferred_element_type=F32)

    def step(c, slot, carry):
        m, l, alpha_prev, m_chunk = carry
        m_chunk_next = scores(c + 1, 1 - slot)
        m, l, alpha = softmax(slot, m_chunk, m, l)
        values(c - 1, 1 - slot, alpha_prev)
        return m, l, alpha, m_chunk_next

    acc_scr[...] = jnp.zeros_like(acc_scr)
    m_chunk = scores(0, 0)
    m_chunk_next = scores(1, 1)
    m, l, alpha = softmax(0, m_chunk, jnp.full((1, tq), -jnp.inf, F32), jnp.zeros((1, tq), F32))
    carry = (m, l, alpha, m_chunk_next)

    def pair(j, carry):
        c = 1 + 2 * j
        return step(c + 1, 0, step(c, 1, carry))

    for j in range((n_chunks - 2) // 2):
        carry = pair(j, carry)
    m, l, alpha_prev, m_chunk = carry
    last = n_chunks - 1
    m, l, alpha = softmax(1, m_chunk, m, l)
    values(last - 1, 0, alpha_prev)
    values(last, 1, alpha)
    o = (acc_scr[...] * (1.0 / l)).T
    o_ref[...] = (o * g_ref[...].astype(F32)).astype(BF16)


def _attention(q_t, k, v_t, gate_a):
    tq, tk = 512, 512
    nq = SEQ // tq
    assert (SEQ // tk) % 2 == 0
    return pl.pallas_call(
        functools.partial(_attn_kernel, tk=tk),
        scratch_shapes=[
            pltpu.VMEM((2, tk, tq), F32),
            pltpu.VMEM((2, tk, tq), BF16),
            pltpu.VMEM((V_HEAD_DIM, tq), F32),
        ],
        out_shape=jax.ShapeDtypeStruct((TOKENS, D_ATTN), BF16),
        grid=(BATCH, N_HEADS, nq),
        in_specs=[
            pl.BlockSpec((1, 1, QK_PAD_DIM, tq), lambda b, h, i: (b, h, 0, i)),
            pl.BlockSpec((1, 1, SEQ, QK_PAD_DIM), lambda b, h, i: (b, h, 0, 0)),
            pl.BlockSpec((1, 1, V_HEAD_DIM, SEQ), lambda b, h, i: (b, h, 0, 0)),
            pl.BlockSpec((tq, V_HEAD_DIM), lambda b, h, i: (b * nq + i, h)),
        ],
        out_specs=pl.BlockSpec((tq, V_HEAD_DIM), lambda b, h, i: (b * nq + i, h)),
        compiler_params=pltpu.CompilerParams(
            dimension_semantics=("arbitrary", "arbitrary", "arbitrary"),
            vmem_limit_bytes=VMEM_LIMIT),
        name="attention",
    )(q_t, k, v_t, gate_a)


OUT_SUB_ROWS = 256


def _out_kernel(yf_ref, ya_ref, x_ref, gate_ref, w_ref, g_ref, b_ref, o_ref):
    for rows in (slice(i * OUT_SUB_ROWS, (i + 1) * OUT_SUB_ROWS)
                 for i in range(x_ref.shape[0] // OUT_SUB_ROWS)):
        y_f = jnp.concatenate([yf_ref[0, g, rows, :] for g in range(FOURIER_GROUPS)], axis=1)
        y = (jnp.dot(y_f, w_ref[:D_FOURIER, :], preferred_element_type=F32)
             + jnp.dot(ya_ref[rows, :], w_ref[D_FOURIER:, :], preferred_element_type=F32))
        r = DEEPNORM_ALPHA * x_ref[rows, :] + gate_ref[0] * y
        mu = jnp.mean(r, axis=-1, keepdims=True)
        rc = r - mu
        var = jnp.mean(rc * rc, axis=-1, keepdims=True)
        o_ref[rows, :] = rc * lax.rsqrt(var + NORM_EPS) * g_ref[...] + b_ref[...]


def _output_stage(y_f, y_a, x2d, gate, w_out_bf16, ln_g, ln_b):
    tm = 512
    nt = SEQ // tm
    row = lambda i: (i, 0)
    return pl.pallas_call(
        _out_kernel,
        out_shape=jax.ShapeDtypeStruct((TOKENS, D_MODEL), F32),
        grid=(TOKENS // tm,),
        in_specs=[
            pl.BlockSpec((1, FOURIER_GROUPS, tm, FOURIER_GROUP_DIM),
                         lambda i: (i // nt, 0, i % nt, 0)),
            pl.BlockSpec((tm, D_ATTN), row),
            pl.BlockSpec((tm, D_MODEL), row),
            pl.BlockSpec((1, 1, D_MODEL), lambda i: (i // nt, 0, 0)),
            _resident((D_MODEL, D_MODEL)),
            _resident((1, D_MODEL)),
            _resident((1, D_MODEL)),
        ],
        out_specs=pl.BlockSpec((tm, D_MODEL), row),
        compiler_params=pltpu.CompilerParams(
            dimension_semantics=("arbitrary",), vmem_limit_bytes=VMEM_LIMIT),
        name="output_stage",
    )(y_f, y_a, x2d, gate, w_out_bf16, ln_g, ln_b)


def _layer_weights(w_in_l, q_norm_l, w_q_b_l, kv_norm_l, w_kv_b_l, w_fmix_l):
    u_f, z_f, cq, ckv, k_r, z_a = jnp.split(
        w_in_l.astype(BF16), [1024, 2048, 2560, 2816, 2880], axis=1)
    pad = jnp.zeros((D_MODEL, W_IN_COLS - w_in_l.shape[1]), BF16)
    w_in = jnp.concatenate([u_f, z_f, z_a, cq, ckv, k_r, pad], axis=1)
    wq = w_q_b_l.reshape(Q_LORA_RANK, N_HEADS, QK_HEAD_DIM)
    wq = jnp.concatenate([
        wq[:, :, :QK_NOPE_DIM].reshape(Q_LORA_RANK, -1),
        wq[:, :, QK_NOPE_DIM:QK_NOPE_DIM + HALF_ROPE].reshape(Q_LORA_RANK, -1),
        wq[:, :, QK_NOPE_DIM + HALF_ROPE:].reshape(Q_LORA_RANK, -1)], axis=1)
    wkv = w_kv_b_l.reshape(KV_LORA_RANK, N_HEADS, QK_NOPE_DIM + V_HEAD_DIM)
    wk = wkv[:, :, :QK_NOPE_DIM].reshape(KV_LORA_RANK, -1)
    wv = wkv[:, :, QK_NOPE_DIM:].reshape(KV_LORA_RANK, -1)
    return {
        "w_in": w_in,
        "q_norm": q_norm_l.reshape(1, Q_LORA_RANK),
        "kv_norm": kv_norm_l.reshape(1, KV_LORA_RANK),
        "wq_t": wq.T.astype(BF16),
        "wk": wk.astype(BF16),
        "wv_t": wv.T.astype(BF16),
        "ab": _fold_fourier_weights(w_fmix_l),
    }


def kernel(x, c, positions, w_ada, b_ada, w_in, q_norm, w_q_b, kv_norm, w_kv_b, w_fmix, w_out,
           ln_g, ln_b):
    assert x.shape == (BATCH, SEQ, D_MODEL) and w_ada.shape[0] == DEPTH
    mod = _ada_modulation(c, w_ada, b_ada)
    tables = _rope_tables(positions)
    dft_consts = _dft_constants()
    x2d = x.reshape(TOKENS, D_MODEL)
    for l in range(DEPTH):
        shift, scale, gate = (
            mod[l, :BATCH, i * D_MODEL:(i + 1) * D_MODEL].reshape(BATCH, 1, D_MODEL)
            for i in range(3))
        w = _layer_weights(w_in[l], q_norm[l], w_q_b[l], kv_norm[l], w_kv_b[l], w_fmix[l])
        pq, gate_f, gate_a, q_t, k, v_t = _token_stage(x2d, scale, shift, tables, w)
        y_f = _seq_dft(pq, gate_f, dft_consts)
        y_a = _attention(q_t, k, v_t, gate_a)
        x2d = _output_stage(y_f, y_a, x2d, gate, w_out[l].astype(BF16),
                            ln_g[l].reshape(1, D_MODEL), ln_b[l].reshape(1, D_MODEL))
    return x2d.reshape(BATCH, SEQ, D_MODEL)
```

```python
import functools
import math

import numpy as np
import jax
import jax.numpy as jnp
from jax import lax
from jax.experimental import pallas as pl
from jax.experimental.pallas import tpu as pltpu

D_MODEL = 2048
BATCH = 4
SEQ = 4096
DEPTH = 2
D_FOURIER = D_MODEL // 2
FOURIER_GROUPS = 8
FOURIER_GROUP_DIM = D_FOURIER // FOURIER_GROUPS
V_HEAD_DIM = 128
D_ATTN = D_MODEL // 2
N_HEADS = D_ATTN // V_HEAD_DIM
QK_NOPE_DIM = 128
QK_ROPE_DIM = 64
QK_HEAD_DIM = QK_NOPE_DIM + QK_ROPE_DIM
Q_LORA_RANK = D_MODEL // 4
KV_LORA_RANK = D_MODEL // 8
ROPE_THETA = 10000.0
NORM_EPS = 1e-6
DEEPNORM_ALPHA = (2 * DEPTH) ** 0.25

TOKENS = BATCH * SEQ
HALF_ROPE = QK_ROPE_DIM // 2
QK_PAD_DIM = 256
W_IN_COLS = 3968
COL_U, COL_ZF, COL_ZA, COL_CQ, COL_CKV = 0, 1024, 2048, 3072, 3584

F32 = jnp.float32
BF16 = jnp.bfloat16
VMEM_LIMIT = 52 * 1024 * 1024

Q_SCALE = math.log2(math.e) / math.sqrt(QK_HEAD_DIM)


def _silu(v):
    return v * (1.0 / (1.0 + jnp.exp(-v)))


def _nt_dot(a, b):
    return lax.dot_general(a, b, (((1,), (1,)), ((), ())), preferred_element_type=F32)


def _ada_kernel(c_ref, w_ref, b_ref, o_ref):
    c_act = _silu(c_ref[...]).astype(BF16)
    w = w_ref[0].astype(BF16)
    o_ref[0] = jnp.dot(c_act, w, preferred_element_type=F32) + b_ref[0]


def _ada_modulation(c, w_ada, b_ada):
    tn = 768
    c_pad = jnp.pad(c, ((0, 8 - BATCH), (0, 0)))
    b3 = b_ada.reshape(DEPTH, 1, 3 * D_MODEL)
    return pl.pallas_call(
        _ada_kernel,
        out_shape=jax.ShapeDtypeStruct((DEPTH, 8, 3 * D_MODEL), F32),
        grid=(DEPTH, 3 * D_MODEL // tn),
        in_specs=[
            pl.BlockSpec((8, D_MODEL), lambda l, j: (0, 0)),
            pl.BlockSpec((1, D_MODEL, tn), lambda l, j: (l, 0, j)),
            pl.BlockSpec((1, 1, tn), lambda l, j: (l, 0, j)),
        ],
        out_specs=pl.BlockSpec((1, 8, tn), lambda l, j: (l, 0, j)),
        compiler_params=pltpu.CompilerParams(
            dimension_semantics=("arbitrary", "arbitrary"), vmem_limit_bytes=VMEM_LIMIT),
        name="ada_modulation",
    )(c_pad, w_ada, b3)


def _rope_kernel(pos_row_ref, pos_col_ref, invf_col_ref, invf_row_ref,
                 cos_t_ref, sin_t_ref, cos2_ref, sin2_ref):
    ang_t = invf_col_ref[...] * pos_row_ref[0].astype(F32)
    cos_t_ref[0] = jnp.cos(ang_t)
    sin_t_ref[0] = jnp.sin(ang_t)
    ang2 = pos_col_ref[0].astype(F32) * invf_row_ref[...]
    lane = lax.broadcasted_iota(jnp.int32, ang2.shape, 1)
    cos2_ref[0] = jnp.cos(ang2)
    sin2_ref[0] = jnp.where(lane < HALF_ROPE, -jnp.sin(ang2), jnp.sin(ang2))


def _rope_tables(positions):
    ts = 1024
    inv_freq = ROPE_THETA ** (-jnp.arange(0, QK_ROPE_DIM, 2, dtype=F32) / QK_ROPE_DIM)
    invf_col = inv_freq.reshape(HALF_ROPE, 1)
    invf_row = jnp.concatenate([inv_freq, inv_freq]).reshape(1, QK_ROPE_DIM)
    pos_row = positions.reshape(BATCH, 1, SEQ)
    pos_col = positions.reshape(BATCH, SEQ, 1)
    return pl.pallas_call(
        _rope_kernel,
        out_shape=(
            jax.ShapeDtypeStruct((BATCH, HALF_ROPE, SEQ), F32),
            jax.ShapeDtypeStruct((BATCH, HALF_ROPE, SEQ), F32),
            jax.ShapeDtypeStruct((BATCH, SEQ, QK_ROPE_DIM), F32),
            jax.ShapeDtypeStruct((BATCH, SEQ, QK_ROPE_DIM), F32),
        ),
        grid=(BATCH, SEQ // ts),
        in_specs=[
            pl.BlockSpec((1, 1, ts), lambda b, s: (b, 0, s)),
            pl.BlockSpec((1, ts, 1), lambda b, s: (b, s, 0)),
            pl.BlockSpec((HALF_ROPE, 1), lambda b, s: (0, 0)),
            pl.BlockSpec((1, QK_ROPE_DIM), lambda b, s: (0, 0)),
        ],
        out_specs=(
            pl.BlockSpec((1, HALF_ROPE, ts), lambda b, s: (b, 0, s)),
            pl.BlockSpec((1, HALF_ROPE, ts), lambda b, s: (b, 0, s)),
            pl.BlockSpec((1, ts, QK_ROPE_DIM), lambda b, s: (b, s, 0)),
            pl.BlockSpec((1, ts, QK_ROPE_DIM), lambda b, s: (b, s, 0)),
        ),
        compiler_params=pltpu.CompilerParams(
            dimension_semantics=("arbitrary", "arbitrary"), vmem_limit_bytes=VMEM_LIMIT),
        name="rope_tables",
    )(pos_row, pos_col, invf_col, invf_row)


def _channel_dft_matrix():
    idx = np.arange(FOURIER_GROUP_DIM)
    ang = 2.0 * np.pi * ((idx[:, None] * idx[None, :]) % FOURIER_GROUP_DIM) / FOURIER_GROUP_DIM
    scale = 1.0 / math.sqrt(FOURIER_GROUP_DIM)
    return np.concatenate([np.cos(ang), np.sin(ang)], axis=1).astype(np.float32) * np.float32(scale)


def _fold_kernel(cs_ref, w_ref, o_ref):
    w = w_ref[0]
    dims = (((1,), (0,)), ((), ()))
    a = lax.dot_general(cs_ref[:, :FOURIER_GROUP_DIM], w, dims,
                        precision=lax.Precision.HIGHEST, preferred_element_type=F32)
    b = lax.dot_general(cs_ref[:, FOURIER_GROUP_DIM:], w, dims,
                        precision=lax.Precision.HIGHEST, preferred_element_type=F32)
    o_ref[0] = jnp.concatenate([a, b], axis=1).astype(BF16)


def _fold_fourier_weights(w_fmix_l):
    g, c = FOURIER_GROUPS, FOURIER_GROUP_DIM
    return pl.pallas_call(
        _fold_kernel,
        out_shape=jax.ShapeDtypeStruct((g, c, 2 * c), BF16),
        grid=(g,),
        in_specs=[
            pl.BlockSpec((c, 2 * c), lambda i: (0, 0)),
            pl.BlockSpec((1, c, c), lambda i: (i, 0, 0)),
        ],
        out_specs=pl.BlockSpec((1, c, 2 * c), lambda i: (i, 0, 0)),
        compiler_params=pltpu.CompilerParams(dimension_semantics=("arbitrary",)),
        name="fold_fourier_weights",
    )(jnp.asarray(_channel_dft_matrix()), w_fmix_l)


def _token_kernel(x_ref, scale_ref, shift_ref, cos_t_ref, sin_t_ref, cos2_ref, sin2_ref,
                  w_in_ref, qn_ref, kvn_ref, wq_t_ref, wk_ref, wv_t_ref, ab_ref,
                  pq_ref, gf_ref, ga_ref, q_t_ref, k_ref, v_t_ref):
    x = x_ref[...]
    tm = x.shape[0]
    mu = jnp.mean(x, axis=-1, keepdims=True)
    xc = x - mu
    var = jnp.mean(xc * xc, axis=-1, keepdims=True)
    h = xc * lax.rsqrt(var + NORM_EPS) * (1.0 + scale_ref[0]) + shift_ref[0]
    hb = h.astype(BF16)

    def proj(lo, hi):
        return jnp.dot(hb, w_in_ref[:, lo:hi], preferred_element_type=F32)

    ub = proj(COL_U, COL_ZF).astype(BF16)
    c = FOURIER_GROUP_DIM
    gate_f = _silu(proj(COL_ZF, COL_ZA)).astype(BF16)
    for g in range(FOURIER_GROUPS):
        pq = jnp.dot(ub[:, g * c:(g + 1) * c], ab_ref[g], preferred_element_type=F32)
        pq_ref[0, 0, g] = pq[:, :c].astype(BF16)
        pq_ref[0, 1, g] = pq[:, c:].astype(BF16)
        gf_ref[0, g] = gate_f[:, g * c:(g + 1) * c]

    ga_ref[...] = _silu(proj(COL_ZA, COL_CQ)).astype(BF16)

    cq = proj(COL_CQ, COL_CKV)
    cqn = (cq * lax.rsqrt(jnp.mean(cq * cq, axis=-1, keepdims=True) + NORM_EPS)
           * qn_ref[...]).astype(BF16)
    q_t = _nt_dot(wq_t_ref[...], cqn) * Q_SCALE
    cos_t = cos_t_ref[0]
    sin_t = sin_t_ref[0]
    n_nope = N_HEADS * QK_NOPE_DIM
    n_half = N_HEADS * HALF_ROPE
    zeros_q = jnp.zeros((QK_PAD_DIM - QK_HEAD_DIM, tm), BF16)
    for hd in range(N_HEADS):
        q_t_ref[0, hd, 0:QK_NOPE_DIM, :] = (
            q_t[hd * QK_NOPE_DIM:(hd + 1) * QK_NOPE_DIM].astype(BF16))
        x1 = q_t[n_nope + hd * HALF_ROPE:n_nope + (hd + 1) * HALF_ROPE]
        x2 = q_t[n_nope + n_half + hd * HALF_ROPE:n_nope + n_half + (hd + 1) * HALF_ROPE]
        q_t_ref[0, hd, QK_NOPE_DIM:QK_NOPE_DIM + HALF_ROPE, :] = (
            x1 * cos_t - x2 * sin_t).astype(BF16)
        q_t_ref[0, hd, QK_NOPE_DIM + HALF_ROPE:QK_HEAD_DIM, :] = (
            x2 * cos_t + x1 * sin_t).astype(BF16)
        q_t_ref[0, hd, QK_HEAD_DIM:QK_PAD_DIM, :] = zeros_q

    ckv_kr = proj(COL_CKV, W_IN_COLS)
    ckv = ckv_kr[:, :KV_LORA_RANK]
    ckvn = (ckv * lax.rsqrt(jnp.mean(ckv * ckv, axis=-1, keepdims=True) + NORM_EPS)
            * kvn_ref[...]).astype(BF16)
    k_nope = jnp.dot(ckvn, wk_ref[...], preferred_element_type=F32)
    v_t = _nt_dot(wv_t_ref[...], ckvn)
    kr = ckv_kr[:, KV_LORA_RANK:KV_LORA_RANK + QK_ROPE_DIM]
    kr_swapped = jnp.concatenate([kr[:, HALF_ROPE:], kr[:, :HALF_ROPE]], axis=1)
    kr_rot = (kr * cos2_ref[0] + kr_swapped * sin2_ref[0]).astype(BF16)
    pad_lane = lax.broadcasted_iota(jnp.int32, (tm, QK_PAD_DIM - QK_HEAD_DIM), 1)
    kr_pad = jnp.concatenate(
        [kr_rot, jnp.where(pad_lane == 0, 1.0, 0.0).astype(BF16)], axis=1)
    for hd in range(N_HEADS):
        k_ref[0, hd, :, 0:QK_NOPE_DIM] = (
            k_nope[:, hd * QK_NOPE_DIM:(hd + 1) * QK_NOPE_DIM].astype(BF16))
        k_ref[0, hd, :, QK_NOPE_DIM:QK_PAD_DIM] = kr_pad
        v_t_ref[0, hd] = v_t[hd * V_HEAD_DIM:(hd + 1) * V_HEAD_DIM].astype(BF16)


def _resident(shape):
    zeros = (0,) * len(shape)
    return pl.BlockSpec(shape, lambda *_: zeros, pipeline_mode=pl.Buffered(1))


def _token_stage(x2d, scale, shift, tables, w):
    tm = 256
    nt = SEQ // tm
    cos_t, sin_t, cos2, sin2 = tables
    row = lambda i: (i, 0)
    per_batch = lambda i: (i // nt, 0, 0)
    bh = lambda i: (i // nt, 0, 0, i % nt)
    return pl.pallas_call(
        _token_kernel,
        out_shape=(
            jax.ShapeDtypeStruct((BATCH, 2, FOURIER_GROUPS, SEQ, FOURIER_GROUP_DIM), BF16),
            jax.ShapeDtypeStruct((BATCH, FOURIER_GROUPS, SEQ, FOURIER_GROUP_DIM), BF16),
            jax.ShapeDtypeStruct((TOKENS, D_ATTN), BF16),
            jax.ShapeDtypeStruct((BATCH, N_HEADS, QK_PAD_DIM, SEQ), BF16),
            jax.ShapeDtypeStruct((BATCH, N_HEADS, SEQ, QK_PAD_DIM), BF16),
            jax.ShapeDtypeStruct((BATCH, N_HEADS, V_HEAD_DIM, SEQ), BF16),
        ),
        grid=(TOKENS // tm,),
        in_specs=[
            pl.BlockSpec((tm, D_MODEL), row),
            pl.BlockSpec((1, 1, D_MODEL), per_batch),
            pl.BlockSpec((1, 1, D_MODEL), per_batch),
            pl.BlockSpec((1, HALF_ROPE, tm), lambda i: (i // nt, 0, i % nt)),
            pl.BlockSpec((1, HALF_ROPE, tm), lambda i: (i // nt, 0, i % nt)),
            pl.BlockSpec((1, tm, QK_ROPE_DIM), lambda i: (i // nt, i % nt, 0)),
            pl.BlockSpec((1, tm, QK_ROPE_DIM), lambda i: (i // nt, i % nt, 0)),
            _resident((D_MODEL, W_IN_COLS)),
            _resident((1, Q_LORA_RANK)),
            _resident((1, KV_LORA_RANK)),
            _resident((N_HEADS * QK_HEAD_DIM, Q_LORA_RANK)),
            _resident((KV_LORA_RANK, N_HEADS * QK_NOPE_DIM)),
            _resident((N_HEADS * V_HEAD_DIM, KV_LORA_RANK)),
            _resident((FOURIER_GROUPS, FOURIER_GROUP_DIM, 2 * FOURIER_GROUP_DIM)),
        ],
        out_specs=(
            pl.BlockSpec((1, 2, FOURIER_GROUPS, tm, FOURIER_GROUP_DIM),
                         lambda i: (i // nt, 0, 0, i % nt, 0)),
            pl.BlockSpec((1, FOURIER_GROUPS, tm, FOURIER_GROUP_DIM),
                         lambda i: (i // nt, 0, i % nt, 0)),
            pl.BlockSpec((tm, D_ATTN), row),
            pl.BlockSpec((1, N_HEADS, QK_PAD_DIM, tm), bh),
            pl.BlockSpec((1, N_HEADS, tm, QK_PAD_DIM), lambda i: (i // nt, 0, i % nt, 0)),
            pl.BlockSpec((1, N_HEADS, V_HEAD_DIM, tm), bh),
        ),
        compiler_params=pltpu.CompilerParams(
            dimension_semantics=("arbitrary",), vmem_limit_bytes=VMEM_LIMIT),
        name="token_stage",
    )(x2d, scale, shift, cos_t, sin_t, cos2, sin2,
      w["w_in"], w["q_norm"], w["kv_norm"], w["wq_t"], w["wk"], w["wv_t"], w["ab"])


DFT_LEVELS = 3
DFT_BLOCKS = 2 ** DFT_LEVELS
DFT_SUB = SEQ // DFT_BLOCKS
DFT_GROUPS_PER_STEP = 2
DFT_ROW_CHUNK = 32


def _dft_constants():
    tw = []
    for level in range(DFT_LEVELS):
        length = SEQ >> level
        ang = 2.0 * np.pi * np.arange(length // 2) / length
        for f in (np.cos, np.sin):
            col = jnp.asarray(f(ang).astype(np.float32)).reshape(-1, 1)
            tw.append(jnp.broadcast_to(col, (length // 2, FOURIER_GROUP_DIM)))
    idx = np.arange(DFT_SUB)
    ang = 2.0 * np.pi * ((idx[:, None] * idx[None, :]) % DFT_SUB) / DFT_SUB
    mat = np.concatenate([np.cos(ang), -np.sin(ang)], axis=1) / math.sqrt(SEQ)
    return tw, jnp.asarray(mat.astype(np.float32)).astype(BF16)


def _bit_reverse(j, bits):
    return int(format(j, "0{}b".format(bits))[::-1], 2)


def _dft_kernel(*refs):
    tw_refs = refs[:2 * DFT_LEVELS]
    m_ref, pq_ref, g_ref, o_ref, p_scr, w_scr = refs[2 * DFT_LEVELS:2 * DFT_LEVELS + 6]
    y_scr = refs[2 * DFT_LEVELS + 6:]
    gps, c, rc = DFT_GROUPS_PER_STEP, FOURIER_GROUP_DIM, DFT_ROW_CHUNK

    def load_input(g, row):
        rows = pl.ds(pl.multiple_of(row, rc), rc)
        return pq_ref[0, 0, g, rows, :].astype(F32), pq_ref[0, 1, g, rows, :].astype(F32)

    def load_scratch(g, row):
        rows = pl.ds(pl.multiple_of(row, rc), rc)
        return p_scr[rows, g * c:(g + 1) * c], w_scr[rows, g * c:(g + 1) * c]

    def store_scratch(g, row, p, w):
        rows = pl.ds(pl.multiple_of(row, rc), rc)
        p_scr[rows, g * c:(g + 1) * c] = p
        w_scr[rows, g * c:(g + 1) * c] = w

    for level in range(DFT_LEVELS):
        half = SEQ >> (level + 1)
        load = load_input if level == 0 else load_scratch
        twc_ref, tws_ref = tw_refs[2 * level], tw_refs[2 * level + 1]

        def body(i, carry, half=half, load=load, twc_ref=twc_ref, tws_ref=tws_ref, level=level):
            r = i * rc
            tw_rows = pl.ds(pl.multiple_of(r, rc), rc)
            cos, sin = twc_ref[tw_rows, :], tws_ref[tw_rows, :]
            for blk in range(2 ** level):
                base = blk * 2 * half
                for g in range(gps):
                    pa, wa = load(g, base + r)
                    pb, wb = load(g, base + half + r)
                    store_scratch(g, base + r, pa + pb, wa + wb)
                    dp, dw = pa - pb, wa - wb
                    store_scratch(g, base + half + r, dp * cos - dw * sin, dw * cos + dp * sin)
            return carry

        lax.fori_loop(0, half // rc, body, 0)

    for j in range(DFT_BLOCKS):
        rows = slice(j * DFT_SUB, (j + 1) * DFT_SUB)
        z = jnp.concatenate([p_scr[rows, :], w_scr[rows, :]], axis=0).astype(BF16)
        y = jnp.dot(m_ref[...], z, preferred_element_type=F32)
        out_rows = pl.ds(_bit_reverse(j, DFT_LEVELS), DFT_SUB, stride=DFT_BLOCKS)
        for g in range(gps):
            y_scr[g][out_rows, :] = y[:, g * c:(g + 1) * c]
    for g in range(gps):
        o_ref[0, g] = (y_scr[g][...] * g_ref[0, g].astype(F32)).astype(BF16)


def _seq_dft(pq, gate_f, consts):
    tw, mat = consts
    gps, c = DFT_GROUPS_PER_STEP, FOURIER_GROUP_DIM
    n_gsteps = FOURIER_GROUPS // gps
    return pl.pallas_call(
        _dft_kernel,
        out_shape=jax.ShapeDtypeStruct((BATCH, FOURIER_GROUPS, SEQ, c), BF16),
        grid=(BATCH, n_gsteps),
        in_specs=(
            [_resident(t.shape) for t in tw]
            + [_resident(mat.shape),
               pl.BlockSpec((1, 2, gps, SEQ, c), lambda b, gi: (b, 0, gi, 0, 0)),
               pl.BlockSpec((1, gps, SEQ, c), lambda b, gi: (b, gi, 0, 0))]),
        out_specs=pl.BlockSpec((1, gps, SEQ, c), lambda b, gi: (b, gi, 0, 0)),
        scratch_shapes=(
            [pltpu.VMEM((SEQ, gps * c), F32)] * 2 + [pltpu.VMEM((SEQ, c), F32)] * gps),
        compiler_params=pltpu.CompilerParams(
            dimension_semantics=("arbitrary", "arbitrary"), vmem_limit_bytes=VMEM_LIMIT),
        name="seq_dft",
    )(*tw, mat, pq, gate_f)


ATTN_PROBE_KEYS = 128
ATTN_MAX_DENOM = 2.0 ** 16


def _attn_kernel(q_t_ref, k_ref, v_t_ref, g_ref, o_ref, q_aug, s_scr, p_scr, *, tk):
    tq = q_t_ref.shape[3]
    n_chunks = SEQ // tk

    def finish(acc, l):
        o = (acc * (1.0 / l)).T
        o_ref[...] = (o * g_ref[...].astype(F32)).astype(BF16)

    s_probe = jnp.dot(k_ref[0, 0, 0:ATTN_PROBE_KEYS, :], q_t_ref[0, 0], preferred_element_type=F32)
    stab = jnp.max(s_probe, axis=0, keepdims=True)
    q_aug[...] = q_t_ref[0, 0]
    pad_rows = QK_PAD_DIM - QK_HEAD_DIM
    first = lax.broadcasted_iota(jnp.int32, (pad_rows, tq), 0) == 0
    q_aug[QK_HEAD_DIM:QK_PAD_DIM, :] = jnp.where(first, -stab, 0.0).astype(BF16)
    def scores(c):
        s_scr[c % 2] = jnp.dot(k_ref[0, 0, c * tk:(c + 1) * tk, :], q_aug[...],
                               preferred_element_type=F32)

    def probs(c):
        p = jnp.exp2(s_scr[c % 2])
        p_scr[c % 2] = p.astype(BF16)
        return jnp.sum(p, axis=0, keepdims=True)

    def values(c):
        return jnp.dot(v_t_ref[0, 0, :, c * tk:(c + 1) * tk], p_scr[c % 2],
                       preferred_element_type=F32)

    scores(0)
    scores(1)
    l = probs(0)
    acc = jnp.zeros((V_HEAD_DIM, tq), F32)
    for c in range(1, n_chunks):
        if c + 1 < n_chunks:
            scores(c + 1)
        l = l + probs(c)
        acc = acc + values(c - 1)
    acc = acc + values(n_chunks - 1)
    finish(acc, l)

    @pl.when(jnp.logical_not(jnp.max(l) <= ATTN_MAX_DENOM))
    def _():
        m = jnp.full((1, tq), -jnp.inf, F32)
        l2 = jnp.zeros((1, tq), F32)
        acc2 = jnp.zeros((V_HEAD_DIM, tq), F32)
        for c in range(n_chunks):
            s = jnp.dot(k_ref[0, 0, c * tk:(c + 1) * tk, :], q_t_ref[0, 0],
                        preferred_element_type=F32)
            m_new = jnp.maximum(m, jnp.max(s, axis=0, keepdims=True))
            alpha = jnp.exp2(m - m_new)
            p = jnp.exp2(s - m_new)
            l2 = alpha * l2 + jnp.sum(p, axis=0, keepdims=True)
            acc2 = alpha * acc2 + jnp.dot(v_t_ref[0, 0, :, c * tk:(c + 1) * tk], p.astype(BF16),
                                          preferred_element_type=F32)
            m = m_new
        finish(acc2, l2)


def _attention(q_t, k, v_t, gate_a):
    tq, tk = 512, 512
    nq = SEQ // tq
    return pl.pallas_call(
        functools.partial(_attn_kernel, tk=tk),
        scratch_shapes=[
            pltpu.VMEM((QK_PAD_DIM, tq), BF16),
            pltpu.VMEM((2, tk, tq), F32),
            pltpu.VMEM((2, tk, tq), BF16),
        ],
        out_shape=jax.ShapeDtypeStruct((TOKENS, D_ATTN), BF16),
        grid=(BATCH, N_HEADS, nq),
        in_specs=[
            pl.BlockSpec((1, 1, QK_PAD_DIM, tq), lambda b, h, i: (b, h, 0, i)),
            pl.BlockSpec((1, 1, SEQ, QK_PAD_DIM), lambda b, h, i: (b, h, 0, 0)),
            pl.BlockSpec((1, 1, V_HEAD_DIM, SEQ), lambda b, h, i: (b, h, 0, 0)),
            pl.BlockSpec((tq, V_HEAD_DIM), lambda b, h, i: (b * nq + i, h)),
        ],
        out_specs=pl.BlockSpec((tq, V_HEAD_DIM), lambda b, h, i: (b * nq + i, h)),
        compiler_params=pltpu.CompilerParams(
            dimension_semantics=("arbitrary", "arbitrary", "arbitrary"),
            vmem_limit_bytes=VMEM_LIMIT),
        name="attention",
    )(q_t, k, v_t, gate_a)


OUT_SUB_ROWS = 256


def _out_kernel(yf_ref, ya_ref, x_ref, gate_ref, w_ref, g_ref, b_ref, o_ref):
    for rows in (slice(i * OUT_SUB_ROWS, (i + 1) * OUT_SUB_ROWS)
                 for i in range(x_ref.shape[0] // OUT_SUB_ROWS)):
        y_f = jnp.concatenate([yf_ref[0, g, rows, :] for g in range(FOURIER_GROUPS)], axis=1)
        y = (jnp.dot(y_f, w_ref[:D_FOURIER, :], preferred_element_type=F32)
             + jnp.dot(ya_ref[rows, :], w_ref[D_FOURIER:, :], preferred_element_type=F32))
        r = DEEPNORM_ALPHA * x_ref[rows, :] + gate_ref[0] * y
        mu = jnp.mean(r, axis=-1, keepdims=True)
        rc = r - mu
        var = jnp.mean(rc * rc, axis=-1, keepdims=True)
        o_ref[rows, :] = rc * lax.rsqrt(var + NORM_EPS) * g_ref[...] + b_ref[...]


def _output_stage(y_f, y_a, x2d, gate, w_out_bf16, ln_g, ln_b):
    tm = 512
    nt = SEQ // tm
    row = lambda i: (i, 0)
    return pl.pallas_call(
        _out_kernel,
        out_shape=jax.ShapeDtypeStruct((TOKENS, D_MODEL), F32),
        grid=(TOKENS // tm,),
        in_specs=[
            pl.BlockSpec((1, FOURIER_GROUPS, tm, FOURIER_GROUP_DIM),
                         lambda i: (i // nt, 0, i % nt, 0)),
            pl.BlockSpec((tm, D_ATTN), row),
            pl.BlockSpec((tm, D_MODEL), row),
            pl.BlockSpec((1, 1, D_MODEL), lambda i: (i // nt, 0, 0)),
            _resident((D_MODEL, D_MODEL)),
            _resident((1, D_MODEL)),
            _resident((1, D_MODEL)),
        ],
        out_specs=pl.BlockSpec((tm, D_MODEL), row),
        compiler_params=pltpu.CompilerParams(
            dimension_semantics=("arbitrary",), vmem_limit_bytes=VMEM_LIMIT),
        name="output_stage",
    )(y_f, y_a, x2d, gate, w_out_bf16, ln_g, ln_b)


def _layer_weights(w_in_l, q_norm_l, w_q_b_l, kv_norm_l, w_kv_b_l, w_fmix_l):
    u_f, z_f, cq, ckv, k_r, z_a = jnp.split(
        w_in_l.astype(BF16), [1024, 2048, 2560, 2816, 2880], axis=1)
    pad = jnp.zeros((D_MODEL, W_IN_COLS - w_in_l.shape[1]), BF16)
    w_in = jnp.concatenate([u_f, z_f, z_a, cq, ckv, k_r, pad], axis=1)
    wq = w_q_b_l.reshape(Q_LORA_RANK, N_HEADS, QK_HEAD_DIM)
    wq = jnp.concatenate([
        wq[:, :, :QK_NOPE_DIM].reshape(Q_LORA_RANK, -1),
        wq[:, :, QK_NOPE_DIM:QK_NOPE_DIM + HALF_ROPE].reshape(Q_LORA_RANK, -1),
        wq[:, :, QK_NOPE_DIM + HALF_ROPE:].reshape(Q_LORA_RANK, -1)], axis=1)
    wkv = w_kv_b_l.reshape(KV_LORA_RANK, N_HEADS, QK_NOPE_DIM + V_HEAD_DIM)
    wk = wkv[:, :, :QK_NOPE_DIM].reshape(KV_LORA_RANK, -1)
    wv = wkv[:, :, QK_NOPE_DIM:].reshape(KV_LORA_RANK, -1)
    return {
        "w_in": w_in,
        "q_norm": q_norm_l.reshape(1, Q_LORA_RANK),
        "kv_norm": kv_norm_l.reshape(1, KV_LORA_RANK),
        "wq_t": wq.T.astype(BF16),
        "wk": wk.astype(BF16),
        "wv_t": wv.T.astype(BF16),
        "ab": _fold_fourier_weights(w_fmix_l),
    }


def kernel(x, c, positions, w_ada, b_ada, w_in, q_norm, w_q_b, kv_norm, w_kv_b, w_fmix, w_out,
           ln_g, ln_b):
    assert x.shape == (BATCH, SEQ, D_MODEL) and w_ada.shape[0] == DEPTH
    mod = _ada_modulation(c, w_ada, b_ada)
    tables = _rope_tables(positions)
    dft_consts = _dft_constants()
    x2d = x.reshape(TOKENS, D_MODEL)
    for l in range(DEPTH):
        shift, scale, gate = (
            mod[l, :BATCH, i * D_MODEL:(i + 1) * D_MODEL].reshape(BATCH, 1, D_MODEL)
            for i in range(3))
        w = _layer_weights(w_in[l], q_norm[l], w_q_b[l], kv_norm[l], w_kv_b[l], w_fmix[l])
        pq, gate_f, gate_a, q_t, k, v_t = _token_stage(x2d, scale, shift, tables, w)
        y_f = _seq_dft(pq, gate_f, dft_consts)
        y_a = _attention(q_t, k, v_t, gate_a)
        x2d = _output_stage(y_f, y_a, x2d, gate, w_out[l].astype(BF16),
                            ln_g[l].reshape(1, D_MODEL), ln_b[l].reshape(1, D_MODEL))
    return x2d.reshape(BATCH, SEQ, D_MODEL)
```

```python
import functools
import math

import numpy as np
import jax
import jax.numpy as jnp
from jax import lax
from jax.experimental import pallas as pl
from jax.experimental.pallas import tpu as pltpu

D_MODEL = 2048
BATCH = 4
SEQ = 4096
DEPTH = 2
D_FOURIER = D_MODEL // 2
FOURIER_GROUPS = 8
FOURIER_GROUP_DIM = D_FOURIER // FOURIER_GROUPS
V_HEAD_DIM = 128
D_ATTN = D_MODEL // 2
N_HEADS = D_ATTN // V_HEAD_DIM
QK_NOPE_DIM = 128
QK_ROPE_DIM = 64
QK_HEAD_DIM = QK_NOPE_DIM + QK_ROPE_DIM
Q_LORA_RANK = D_MODEL // 4
KV_LORA_RANK = D_MODEL // 8
ROPE_THETA = 10000.0
NORM_EPS = 1e-6
DEEPNORM_ALPHA = (2 * DEPTH) ** 0.25

TOKENS = BATCH * SEQ
HALF_ROPE = QK_ROPE_DIM // 2
QK_PAD_DIM = 256
W_IN_COLS = 3968
COL_U, COL_ZF, COL_CQ, COL_CKV, COL_ZA = 0, 1024, 2048, 2560, 2944

F32 = jnp.float32
BF16 = jnp.bfloat16
VMEM_LIMIT = 52 * 1024 * 1024

Q_SCALE = math.log2(math.e) / math.sqrt(QK_HEAD_DIM)


def _silu(v):
    return v * (1.0 / (1.0 + jnp.exp(-v)))


def _nt_dot(a, b):
    return lax.dot_general(a, b, (((1,), (1,)), ((), ())), preferred_element_type=F32)


def _ada_kernel(c_ref, w_ref, b_ref, o_ref):
    c_act = _silu(c_ref[...]).astype(BF16)
    w = w_ref[0].astype(BF16)
    o_ref[0] = jnp.dot(c_act, w, preferred_element_type=F32) + b_ref[0]


def _ada_modulation(c, w_ada, b_ada):
    tn = 768
    c_pad = jnp.pad(c, ((0, 8 - BATCH), (0, 0)))
    b3 = b_ada.reshape(DEPTH, 1, 3 * D_MODEL)
    return pl.pallas_call(
        _ada_kernel,
        out_shape=jax.ShapeDtypeStruct((DEPTH, 8, 3 * D_MODEL), F32),
        grid=(DEPTH, 3 * D_MODEL // tn),
        in_specs=[
            pl.BlockSpec((8, D_MODEL), lambda l, j: (0, 0)),
            pl.BlockSpec((1, D_MODEL, tn), lambda l, j: (l, 0, j)),
            pl.BlockSpec((1, 1, tn), lambda l, j: (l, 0, j)),
        ],
        out_specs=pl.BlockSpec((1, 8, tn), lambda l, j: (l, 0, j)),
        compiler_params=pltpu.CompilerParams(
            dimension_semantics=("arbitrary", "arbitrary"), vmem_limit_bytes=VMEM_LIMIT),
        name="ada_modulation",
    )(c_pad, w_ada, b3)


def _rope_kernel(pos_row_ref, pos_col_ref, invf_col_ref, invf_row_ref,
                 cos_t_ref, sin_t_ref, cos2_ref, sin2_ref):
    ang_t = invf_col_ref[...] * pos_row_ref[0].astype(F32)
    cos_t_ref[0] = jnp.cos(ang_t)
    sin_t_ref[0] = jnp.sin(ang_t)
    ang2 = pos_col_ref[0].astype(F32) * invf_row_ref[...]
    lane = lax.broadcasted_iota(jnp.int32, ang2.shape, 1)
    cos2_ref[0] = jnp.cos(ang2)
    sin2_ref[0] = jnp.where(lane < HALF_ROPE, -jnp.sin(ang2), jnp.sin(ang2))


def _rope_tables(positions):
    ts = 1024
    inv_freq = ROPE_THETA ** (-jnp.arange(0, QK_ROPE_DIM, 2, dtype=F32) / QK_ROPE_DIM)
    invf_col = inv_freq.reshape(HALF_ROPE, 1)
    invf_row = jnp.concatenate([inv_freq, inv_freq]).reshape(1, QK_ROPE_DIM)
    pos_row = positions.reshape(BATCH, 1, SEQ)
    pos_col = positions.reshape(BATCH, SEQ, 1)
    return pl.pallas_call(
        _rope_kernel,
        out_shape=(
            jax.ShapeDtypeStruct((BATCH, HALF_ROPE, SEQ), F32),
            jax.ShapeDtypeStruct((BATCH, HALF_ROPE, SEQ), F32),
            jax.ShapeDtypeStruct((BATCH, SEQ, QK_ROPE_DIM), F32),
            jax.ShapeDtypeStruct((BATCH, SEQ, QK_ROPE_DIM), F32),
        ),
        grid=(BATCH, SEQ // ts),
        in_specs=[
            pl.BlockSpec((1, 1, ts), lambda b, s: (b, 0, s)),
            pl.BlockSpec((1, ts, 1), lambda b, s: (b, s, 0)),
            pl.BlockSpec((HALF_ROPE, 1), lambda b, s: (0, 0)),
            pl.BlockSpec((1, QK_ROPE_DIM), lambda b, s: (0, 0)),
        ],
        out_specs=(
            pl.BlockSpec((1, HALF_ROPE, ts), lambda b, s: (b, 0, s)),
            pl.BlockSpec((1, HALF_ROPE, ts), lambda b, s: (b, 0, s)),
            pl.BlockSpec((1, ts, QK_ROPE_DIM), lambda b, s: (b, s, 0)),
            pl.BlockSpec((1, ts, QK_ROPE_DIM), lambda b, s: (b, s, 0)),
        ),
        compiler_params=pltpu.CompilerParams(
            dimension_semantics=("arbitrary", "arbitrary"), vmem_limit_bytes=VMEM_LIMIT),
        name="rope_tables",
    )(pos_row, pos_col, invf_col, invf_row)


def _channel_dft_matrix():
    idx = np.arange(FOURIER_GROUP_DIM)
    ang = 2.0 * np.pi * ((idx[:, None] * idx[None, :]) % FOURIER_GROUP_DIM) / FOURIER_GROUP_DIM
    scale = 1.0 / math.sqrt(FOURIER_GROUP_DIM)
    return np.concatenate([np.cos(ang), np.sin(ang)], axis=1).astype(np.float32) * np.float32(scale)


def _fold_kernel(cs_ref, w_ref, o_ref):
    w = w_ref[0]
    dims = (((1,), (0,)), ((), ()))
    a = lax.dot_general(cs_ref[:, :FOURIER_GROUP_DIM], w, dims,
                        precision=lax.Precision.HIGHEST, preferred_element_type=F32)
    b = lax.dot_general(cs_ref[:, FOURIER_GROUP_DIM:], w, dims,
                        precision=lax.Precision.HIGHEST, preferred_element_type=F32)
    o_ref[0] = jnp.concatenate([a, b], axis=1).astype(BF16)


def _fold_fourier_weights(w_fmix_l):
    g, c = FOURIER_GROUPS, FOURIER_GROUP_DIM
    return pl.pallas_call(
        _fold_kernel,
        out_shape=jax.ShapeDtypeStruct((g, c, 2 * c), BF16),
        grid=(g,),
        in_specs=[
            pl.BlockSpec((c, 2 * c), lambda i: (0, 0)),
            pl.BlockSpec((1, c, c), lambda i: (i, 0, 0)),
        ],
        out_specs=pl.BlockSpec((1, c, 2 * c), lambda i: (i, 0, 0)),
        compiler_params=pltpu.CompilerParams(dimension_semantics=("arbitrary",)),
        name="fold_fourier_weights",
    )(jnp.asarray(_channel_dft_matrix()), w_fmix_l)


def _token_kernel(x_ref, scale_ref, shift_ref, cos_t_ref, sin_t_ref, cos2_ref, sin2_ref,
                  w_in_ref, qn_ref, kvn_ref, wq_t_ref, wk_ref, wv_t_ref, ab_ref,
                  pq_ref, gf_ref, ga_ref, q_t_ref, k_ref, v_t_ref):
    x = x_ref[...]
    tm = x.shape[0]
    mu = jnp.mean(x, axis=-1, keepdims=True)
    xc = x - mu
    var = jnp.mean(xc * xc, axis=-1, keepdims=True)
    h = xc * lax.rsqrt(var + NORM_EPS) * (1.0 + scale_ref[0]) + shift_ref[0]
    hb = h.astype(BF16)

    def proj(lo, hi):
        return jnp.dot(hb, w_in_ref[:, lo:hi], preferred_element_type=F32)

    ub = proj(COL_U, COL_ZF).astype(BF16)
    c = FOURIER_GROUP_DIM
    gate_f = _silu(proj(COL_ZF, COL_CQ)).astype(BF16)
    for g in range(FOURIER_GROUPS):
        pq = jnp.dot(ub[:, g * c:(g + 1) * c], ab_ref[g], preferred_element_type=F32)
        pq_ref[0, 0, g] = pq[:, :c].astype(BF16)
        pq_ref[0, 1, g] = pq[:, c:].astype(BF16)
        gf_ref[0, g] = gate_f[:, g * c:(g + 1) * c]

    ga_ref[...] = _silu(proj(COL_ZA, W_IN_COLS)).astype(BF16)

    cq = proj(COL_CQ, COL_CKV)
    cqn = (cq * lax.rsqrt(jnp.mean(cq * cq, axis=-1, keepdims=True) + NORM_EPS)
           * qn_ref[...]).astype(BF16)
    q_t = _nt_dot(wq_t_ref[...], cqn) * Q_SCALE
    cos_t = cos_t_ref[0]
    sin_t = sin_t_ref[0]
    n_nope = N_HEADS * QK_NOPE_DIM
    n_half = N_HEADS * HALF_ROPE
    zeros_q = jnp.zeros((QK_PAD_DIM - QK_HEAD_DIM, tm), BF16)
    for hd in range(N_HEADS):
        q_t_ref[0, hd, 0:QK_NOPE_DIM, :] = (
            q_t[hd * QK_NOPE_DIM:(hd + 1) * QK_NOPE_DIM].astype(BF16))
        x1 = q_t[n_nope + hd * HALF_ROPE:n_nope + (hd + 1) * HALF_ROPE]
        x2 = q_t[n_nope + n_half + hd * HALF_ROPE:n_nope + n_half + (hd + 1) * HALF_ROPE]
        q_t_ref[0, hd, QK_NOPE_DIM:QK_NOPE_DIM + HALF_ROPE, :] = (
            x1 * cos_t - x2 * sin_t).astype(BF16)
        q_t_ref[0, hd, QK_NOPE_DIM + HALF_ROPE:QK_HEAD_DIM, :] = (
            x2 * cos_t + x1 * sin_t).astype(BF16)
        q_t_ref[0, hd, QK_HEAD_DIM:QK_PAD_DIM, :] = zeros_q

    ckv_kr = proj(COL_CKV, COL_ZA)
    ckv = ckv_kr[:, :KV_LORA_RANK]
    ckvn = (ckv * lax.rsqrt(jnp.mean(ckv * ckv, axis=-1, keepdims=True) + NORM_EPS)
            * kvn_ref[...]).astype(BF16)
    k_nope = jnp.dot(ckvn, wk_ref[...], preferred_element_type=F32)
    v_t = _nt_dot(wv_t_ref[...], ckvn)
    kr = ckv_kr[:, KV_LORA_RANK:KV_LORA_RANK + QK_ROPE_DIM]
    kr_swapped = jnp.concatenate([kr[:, HALF_ROPE:], kr[:, :HALF_ROPE]], axis=1)
    kr_rot = (kr * cos2_ref[0] + kr_swapped * sin2_ref[0]).astype(BF16)
    pad_lane = lax.broadcasted_iota(jnp.int32, (tm, QK_PAD_DIM - QK_HEAD_DIM), 1)
    kr_pad = jnp.concatenate(
        [kr_rot, jnp.where(pad_lane == 0, 1.0, 0.0).astype(BF16)], axis=1)
    for hd in range(N_HEADS):
        k_ref[0, hd, :, 0:QK_NOPE_DIM] = (
            k_nope[:, hd * QK_NOPE_DIM:(hd + 1) * QK_NOPE_DIM].astype(BF16))
        k_ref[0, hd, :, QK_NOPE_DIM:QK_PAD_DIM] = kr_pad
        v_t_ref[0, hd] = v_t[hd * V_HEAD_DIM:(hd + 1) * V_HEAD_DIM].astype(BF16)


def _resident(shape):
    zeros = (0,) * len(shape)
    return pl.BlockSpec(shape, lambda *_: zeros, pipeline_mode=pl.Buffered(1))


def _token_stage(x2d, scale, shift, tables, w):
    tm = 256
    nt = SEQ // tm
    cos_t, sin_t, cos2, sin2 = tables
    row = lambda i: (i, 0)
    per_batch = lambda i: (i // nt, 0, 0)
    bh = lambda i: (i // nt, 0, 0, i % nt)
    return pl.pallas_call(
        _token_kernel,
        out_shape=(
            jax.ShapeDtypeStruct((BATCH, 2, FOURIER_GROUPS, SEQ, FOURIER_GROUP_DIM), BF16),
            jax.ShapeDtypeStruct((BATCH, FOURIER_GROUPS, SEQ, FOURIER_GROUP_DIM), BF16),
            jax.ShapeDtypeStruct((TOKENS, D_ATTN), BF16),
            jax.ShapeDtypeStruct((BATCH, N_HEADS, QK_PAD_DIM, SEQ), BF16),
            jax.ShapeDtypeStruct((BATCH, N_HEADS, SEQ, QK_PAD_DIM), BF16),
            jax.ShapeDtypeStruct((BATCH, N_HEADS, V_HEAD_DIM, SEQ), BF16),
        ),
        grid=(TOKENS // tm,),
        in_specs=[
            pl.BlockSpec((tm, D_MODEL), row),
            pl.BlockSpec((1, 1, D_MODEL), per_batch),
            pl.BlockSpec((1, 1, D_MODEL), per_batch),
            pl.BlockSpec((1, HALF_ROPE, tm), lambda i: (i // nt, 0, i % nt)),
            pl.BlockSpec((1, HALF_ROPE, tm), lambda i: (i // nt, 0, i % nt)),
            pl.BlockSpec((1, tm, QK_ROPE_DIM), lambda i: (i // nt, i % nt, 0)),
            pl.BlockSpec((1, tm, QK_ROPE_DIM), lambda i: (i // nt, i % nt, 0)),
            _resident((D_MODEL, W_IN_COLS)),
            _resident((1, Q_LORA_RANK)),
            _resident((1, KV_LORA_RANK)),
            _resident((N_HEADS * QK_HEAD_DIM, Q_LORA_RANK)),
            _resident((KV_LORA_RANK, N_HEADS * QK_NOPE_DIM)),
            _resident((N_HEADS * V_HEAD_DIM, KV_LORA_RANK)),
            _resident((FOURIER_GROUPS, FOURIER_GROUP_DIM, 2 * FOURIER_GROUP_DIM)),
        ],
        out_specs=(
            pl.BlockSpec((1, 2, FOURIER_GROUPS, tm, FOURIER_GROUP_DIM),
                         lambda i: (i // nt, 0, 0, i % nt, 0)),
            pl.BlockSpec((1, FOURIER_GROUPS, tm, FOURIER_GROUP_DIM),
                         lambda i: (i // nt, 0, i % nt, 0)),
            pl.BlockSpec((tm, D_ATTN), row),
            pl.BlockSpec((1, N_HEADS, QK_PAD_DIM, tm), bh),
            pl.BlockSpec((1, N_HEADS, tm, QK_PAD_DIM), lambda i: (i // nt, 0, i % nt, 0)),
            pl.BlockSpec((1, N_HEADS, V_HEAD_DIM, tm), bh),
        ),
        compiler_params=pltpu.CompilerParams(
            dimension_semantics=("arbitrary",), vmem_limit_bytes=VMEM_LIMIT),
        name="token_stage",
    )(x2d, scale, shift, cos_t, sin_t, cos2, sin2,
      w["w_in"], w["q_norm"], w["kv_norm"], w["wq_t"], w["wk"], w["wv_t"], w["ab"])


DFT_LEVELS = 3
DFT_BLOCKS = 2 ** DFT_LEVELS
DFT_SUB = SEQ // DFT_BLOCKS
DFT_GROUPS_PER_STEP = 2
DFT_ROW_CHUNK = 32


def _dft_constants():
    tw = []
    for level in range(DFT_LEVELS):
        length = SEQ >> level
        ang = 2.0 * np.pi * np.arange(length // 2) / length
        for f in (np.cos, np.sin):
            col = jnp.asarray(f(ang).astype(np.float32)).reshape(-1, 1)
            tw.append(jnp.broadcast_to(col, (length // 2, FOURIER_GROUP_DIM)))
    idx = np.arange(DFT_SUB)
    ang = 2.0 * np.pi * ((idx[:, None] * idx[None, :]) % DFT_SUB) / DFT_SUB
    mat = np.concatenate([np.cos(ang), -np.sin(ang)], axis=1) / math.sqrt(SEQ)
    return tw, jnp.asarray(mat.astype(np.float32)).astype(BF16)


def _bit_reverse(j, bits):
    return int(format(j, "0{}b".format(bits))[::-1], 2)


def _dft_kernel(*refs):
    tw_refs = refs[:2 * DFT_LEVELS]
    m_ref, pq_ref, g_ref, o_ref, p_scr, w_scr = refs[2 * DFT_LEVELS:2 * DFT_LEVELS + 6]
    y_scr = refs[2 * DFT_LEVELS + 6:]
    gps, c, rc = DFT_GROUPS_PER_STEP, FOURIER_GROUP_DIM, DFT_ROW_CHUNK

    def load_input(g, row):
        rows = pl.ds(pl.multiple_of(row, rc), rc)
        return pq_ref[0, 0, g, rows, :].astype(F32), pq_ref[0, 1, g, rows, :].astype(F32)

    def load_scratch(g, row):
        rows = pl.ds(pl.multiple_of(row, rc), rc)
        return p_scr[rows, g * c:(g + 1) * c], w_scr[rows, g * c:(g + 1) * c]

    def store_scratch(g, row, p, w):
        rows = pl.ds(pl.multiple_of(row, rc), rc)
        p_scr[rows, g * c:(g + 1) * c] = p
        w_scr[rows, g * c:(g + 1) * c] = w

    for level in range(DFT_LEVELS):
        half = SEQ >> (level + 1)
        load = load_input if level == 0 else load_scratch
        twc_ref, tws_ref = tw_refs[2 * level], tw_refs[2 * level + 1]

        def body(i, carry, half=half, load=load, twc_ref=twc_ref, tws_ref=tws_ref, level=level):
            r = i * rc
            tw_rows = pl.ds(pl.multiple_of(r, rc), rc)
            cos, sin = twc_ref[tw_rows, :], tws_ref[tw_rows, :]
            for blk in range(2 ** level):
                base = blk * 2 * half
                for g in range(gps):
                    pa, wa = load(g, base + r)
                    pb, wb = load(g, base + half + r)
                    store_scratch(g, base + r, pa + pb, wa + wb)
                    dp, dw = pa - pb, wa - wb
                    store_scratch(g, base + half + r, dp * cos - dw * sin, dw * cos + dp * sin)
            return carry

        lax.fori_loop(0, half // rc, body, 0)

    for j in range(DFT_BLOCKS):
        rows = slice(j * DFT_SUB, (j + 1) * DFT_SUB)
        z = jnp.concatenate([p_scr[rows, :], w_scr[rows, :]], axis=0).astype(BF16)
        y = jnp.dot(m_ref[...], z, preferred_element_type=F32)
        out_rows = pl.ds(_bit_reverse(j, DFT_LEVELS), DFT_SUB, stride=DFT_BLOCKS)
        for g in range(gps):
            y_scr[g][out_rows, :] = y[:, g * c:(g + 1) * c]
    for g in range(gps):
        o_ref[0, g] = (y_scr[g][...] * g_ref[0, g].astype(F32)).astype(BF16)


def _seq_dft(pq, gate_f, consts):
    tw, mat = consts
    gps, c = DFT_GROUPS_PER_STEP, FOURIER_GROUP_DIM
    n_gsteps = FOURIER_GROUPS // gps
    return pl.pallas_call(
        _dft_kernel,
        out_shape=jax.ShapeDtypeStruct((BATCH, FOURIER_GROUPS, SEQ, c), BF16),
        grid=(BATCH, n_gsteps),
        in_specs=(
            [_resident(t.shape) for t in tw]
            + [_resident(mat.shape),
               pl.BlockSpec((1, 2, gps, SEQ, c), lambda b, gi: (b, 0, gi, 0, 0)),
               pl.BlockSpec((1, gps, SEQ, c), lambda b, gi: (b, gi, 0, 0))]),
        out_specs=pl.BlockSpec((1, gps, SEQ, c), lambda b, gi: (b, gi, 0, 0)),
        scratch_shapes=(
            [pltpu.VMEM((SEQ, gps * c), F32)] * 2 + [pltpu.VMEM((SEQ, c), F32)] * gps),
        compiler_params=pltpu.CompilerParams(
            dimension_semantics=("arbitrary", "arbitrary"), vmem_limit_bytes=VMEM_LIMIT),
        name="seq_dft",
    )(*tw, mat, pq, gate_f)


ATTN_PROBE_KEYS = 128
ATTN_MAX_DENOM = 2.0 ** 16


def _attn_kernel(q_t_ref, k_ref, v_t_ref, g_ref, o_ref, q_aug, s_scr, p_scr, *, tk):
    tq = q_t_ref.shape[3]
    n_chunks = SEQ // tk

    def finish(acc, l):
        o = (acc * (1.0 / l)).T
        o_ref[...] = (o * g_ref[...].astype(F32)).astype(BF16)

    s_probe = jnp.dot(k_ref[0, 0, 0:ATTN_PROBE_KEYS, :], q_t_ref[0, 0], preferred_element_type=F32)
    stab = jnp.max(s_probe, axis=0, keepdims=True)
    q_aug[...] = q_t_ref[0, 0]
    pad_rows = QK_PAD_DIM - QK_HEAD_DIM
    first = lax.broadcasted_iota(jnp.int32, (pad_rows, tq), 0) == 0
    q_aug[QK_HEAD_DIM:QK_PAD_DIM, :] = jnp.where(first, -stab, 0.0).astype(BF16)
    def scores(c):
        s_scr[c % 2] = jnp.dot(k_ref[0, 0, c * tk:(c + 1) * tk, :], q_aug[...],
                               preferred_element_type=F32)

    def probs(c):
        p = jnp.exp2(s_scr[c % 2])
        p_scr[c % 2] = p.astype(BF16)
        return jnp.sum(p, axis=0, keepdims=True)

    def values(c):
        return jnp.dot(v_t_ref[0, 0, :, c * tk:(c + 1) * tk], p_scr[c % 2],
                       preferred_element_type=F32)

    scores(0)
    scores(1)
    l = probs(0)
    acc = jnp.zeros((V_HEAD_DIM, tq), F32)
    for c in range(1, n_chunks):
        if c + 1 < n_chunks:
            scores(c + 1)
        l = l + probs(c)
        acc = acc + values(c - 1)
    acc = acc + values(n_chunks - 1)
    finish(acc, l)

    @pl.when(jnp.logical_not(jnp.max(l) <= ATTN_MAX_DENOM))
    def _():
        m = jnp.full((1, tq), -jnp.inf, F32)
        l2 = jnp.zeros((1, tq), F32)
        acc2 = jnp.zeros((V_HEAD_DIM, tq), F32)
        for c in range(n_chunks):
            s = jnp.dot(k_ref[0, 0, c * tk:(c + 1) * tk, :], q_t_ref[0, 0],
                        preferred_element_type=F32)
            m_new = jnp.maximum(m, jnp.max(s, axis=0, keepdims=True))
            alpha = jnp.exp2(m - m_new)
            p = jnp.exp2(s - m_new)
            l2 = alpha * l2 + jnp.sum(p, axis=0, keepdims=True)
            acc2 = alpha * acc2 + jnp.dot(v_t_ref[0, 0, :, c * tk:(c + 1) * tk], p.astype(BF16),
                                          preferred_element_type=F32)
            m = m_new
        finish(acc2, l2)


def _attention(q_t, k, v_t, gate_a):
    tq, tk = 1024, 512
    nq = SEQ // tq
    return pl.pallas_call(
        functools.partial(_attn_kernel, tk=tk),
        scratch_shapes=[
            pltpu.VMEM((QK_PAD_DIM, tq), BF16),
            pltpu.VMEM((2, tk, tq), F32),
            pltpu.VMEM((2, tk, tq), BF16),
        ],
        out_shape=jax.ShapeDtypeStruct((TOKENS, D_ATTN), BF16),
        grid=(BATCH, N_HEADS, nq),
        in_specs=[
            pl.BlockSpec((1, 1, QK_PAD_DIM, tq), lambda b, h, i: (b, h, 0, i)),
            pl.BlockSpec((1, 1, SEQ, QK_PAD_DIM), lambda b, h, i: (b, h, 0, 0)),
            pl.BlockSpec((1, 1, V_HEAD_DIM, SEQ), lambda b, h, i: (b, h, 0, 0)),
            pl.BlockSpec((tq, V_HEAD_DIM), lambda b, h, i: (b * nq + i, h)),
        ],
        out_specs=pl.BlockSpec((tq, V_HEAD_DIM), lambda b, h, i: (b * nq + i, h)),
        compiler_params=pltpu.CompilerParams(
            dimension_semantics=("arbitrary", "arbitrary", "arbitrary"),
            vmem_limit_bytes=VMEM_LIMIT),
        name="attention",
    )(q_t, k, v_t, gate_a)


OUT_SUB_ROWS = 256


def _out_kernel(yf_ref, ya_ref, x_ref, gate_ref, w_ref, g_ref, b_ref, o_ref):
    for rows in (slice(i * OUT_SUB_ROWS, (i + 1) * OUT_SUB_ROWS)
                 for i in range(x_ref.shape[0] // OUT_SUB_ROWS)):
        y_f = jnp.concatenate([yf_ref[0, g, rows, :] for g in range(FOURIER_GROUPS)], axis=1)
        y = (jnp.dot(y_f, w_ref[:D_FOURIER, :], preferred_element_type=F32)
             + jnp.dot(ya_ref[rows, :], w_ref[D_FOURIER:, :], preferred_element_type=F32))
        r = DEEPNORM_ALPHA * x_ref[rows, :] + gate_ref[0] * y
        mu = jnp.mean(r, axis=-1, keepdims=True)
        rc = r - mu
        var = jnp.mean(rc * rc, axis=-1, keepdims=True)
        o_ref[rows, :] = rc * lax.rsqrt(var + NORM_EPS) * g_ref[...] + b_ref[...]


def _output_stage(y_f, y_a, x2d, gate, w_out_bf16, ln_g, ln_b):
    tm = 512
    nt = SEQ // tm
    row = lambda i: (i, 0)
    return pl.pallas_call(
        _out_kernel,
        out_shape=jax.ShapeDtypeStruct((TOKENS, D_MODEL), F32),
        grid=(TOKENS // tm,),
        in_specs=[
            pl.BlockSpec((1, FOURIER_GROUPS, tm, FOURIER_GROUP_DIM),
                         lambda i: (i // nt, 0, i % nt, 0)),
            pl.BlockSpec((tm, D_ATTN), row),
            pl.BlockSpec((tm, D_MODEL), row),
            pl.BlockSpec((1, 1, D_MODEL), lambda i: (i // nt, 0, 0)),
            _resident((D_MODEL, D_MODEL)),
            _resident((1, D_MODEL)),
            _resident((1, D_MODEL)),
        ],
        out_specs=pl.BlockSpec((tm, D_MODEL), row),
        compiler_params=pltpu.CompilerParams(
            dimension_semantics=("arbitrary",), vmem_limit_bytes=VMEM_LIMIT),
        name="output_stage",
    )(y_f, y_a, x2d, gate, w_out_bf16, ln_g, ln_b)


def _w_in_kernel(w_ref, o_ref):
    w = w_ref[0]
    pad = W_IN_COLS - w.shape[1]
    split = COL_ZA - pad
    aligned = COL_ZA - 128
    o_ref[0, :, 0:aligned] = w[:, 0:aligned].astype(BF16)
    o_ref[0, :, aligned:COL_ZA] = jnp.concatenate(
        [w[:, aligned:split], jnp.zeros((w.shape[0], pad), F32)], axis=1).astype(BF16)
    o_ref[0, :, COL_ZA:W_IN_COLS] = w[:, split:].astype(BF16)


def _prepare_w_in(w_in):
    rows = 256
    cols = w_in.shape[2]
    return pl.pallas_call(
        _w_in_kernel,
        out_shape=jax.ShapeDtypeStruct((DEPTH, D_MODEL, W_IN_COLS), BF16),
        grid=(DEPTH, D_MODEL // rows),
        in_specs=[pl.BlockSpec((1, rows, cols), lambda l, i: (l, i, 0))],
        out_specs=pl.BlockSpec((1, rows, W_IN_COLS), lambda l, i: (l, i, 0)),
        compiler_params=pltpu.CompilerParams(
            dimension_semantics=("arbitrary", "arbitrary"), vmem_limit_bytes=VMEM_LIMIT),
        name="prepare_w_in",
    )(w_in)


def _layer_weights(w_in_l, q_norm_l, w_q_b_l, kv_norm_l, w_kv_b_l, w_fmix_l):
    w_in = w_in_l
    wq = w_q_b_l.reshape(Q_LORA_RANK, N_HEADS, QK_HEAD_DIM)
    wq = jnp.concatenate([
        wq[:, :, :QK_NOPE_DIM].reshape(Q_LORA_RANK, -1),
        wq[:, :, QK_NOPE_DIM:QK_NOPE_DIM + HALF_ROPE].reshape(Q_LORA_RANK, -1),
        wq[:, :, QK_NOPE_DIM + HALF_ROPE:].reshape(Q_LORA_RANK, -1)], axis=1)
    wkv = w_kv_b_l.reshape(KV_LORA_RANK, N_HEADS, QK_NOPE_DIM + V_HEAD_DIM)
    wk = wkv[:, :, :QK_NOPE_DIM].reshape(KV_LORA_RANK, -1)
    wv = wkv[:, :, QK_NOPE_DIM:].reshape(KV_LORA_RANK, -1)
    return {
        "w_in": w_in,
        "q_norm": q_norm_l.reshape(1, Q_LORA_RANK),
        "kv_norm": kv_norm_l.reshape(1, KV_LORA_RANK),
        "wq_t": wq.T.astype(BF16),
        "wk": wk.astype(BF16),
        "wv_t": wv.T.astype(BF16),
        "ab": _fold_fourier_weights(w_fmix_l),
    }


def kernel(x, c, positions, w_ada, b_ada, w_in, q_norm, w_q_b, kv_norm, w_kv_b, w_fmix, w_out,
           ln_g, ln_b):
    assert x.shape == (BATCH, SEQ, D_MODEL) and w_ada.shape[0] == DEPTH
    mod = _ada_modulation(c, w_ada, b_ada)
    tables = _rope_tables(positions)
    dft_consts = _dft_constants()
    w_in = _prepare_w_in(w_in)
    x2d = x.reshape(TOKENS, D_MODEL)
    for l in range(DEPTH):
        shift, scale, gate = (
            mod[l, :BATCH, i * D_MODEL:(i + 1) * D_MODEL].reshape(BATCH, 1, D_MODEL)
            for i in range(3))
        w = _layer_weights(w_in[l], q_norm[l], w_q_b[l], kv_norm[l], w_kv_b[l], w_fmix[l])
        pq, gate_f, gate_a, q_t, k, v_t = _token_stage(x2d, scale, shift, tables, w)
        y_f = _seq_dft(pq, gate_f, dft_consts)
        y_a = _attention(q_t, k, v_t, gate_a)
        x2d = _output_stage(y_f, y_a, x2d, gate, w_out[l].astype(BF16),
                            ln_g[l].reshape(1, D_MODEL), ln_b[l].reshape(1, D_MODEL))
    return x2d.reshape(BATCH, SEQ, D_MODEL)
```

```python
import functools
import math

import numpy as np
import jax
import jax.numpy as jnp
from jax import lax
from jax.experimental import pallas as pl
from jax.experimental.pallas import tpu as pltpu

D_MODEL = 2048
BATCH = 4
SEQ = 4096
DEPTH = 2
D_FOURIER = D_MODEL // 2
FOURIER_GROUPS = 8
FOURIER_GROUP_DIM = D_FOURIER // FOURIER_GROUPS
V_HEAD_DIM = 128
D_ATTN = D_MODEL // 2
N_HEADS = D_ATTN // V_HEAD_DIM
QK_NOPE_DIM = 128
QK_ROPE_DIM = 64
QK_HEAD_DIM = QK_NOPE_DIM + QK_ROPE_DIM
Q_LORA_RANK = D_MODEL // 4
KV_LORA_RANK = D_MODEL // 8
ROPE_THETA = 10000.0
NORM_EPS = 1e-6
DEEPNORM_ALPHA = (2 * DEPTH) ** 0.25

TOKENS = BATCH * SEQ
HALF_ROPE = QK_ROPE_DIM // 2
QK_PAD_DIM = 256
W_IN_COLS = 3968
COL_U, COL_ZF, COL_CQ, COL_CKV, COL_ZA = 0, 1024, 2048, 2560, 2944

F32 = jnp.float32
BF16 = jnp.bfloat16
VMEM_LIMIT = 52 * 1024 * 1024

Q_SCALE = math.log2(math.e) / math.sqrt(QK_HEAD_DIM)


def _silu(v):
    return v * (1.0 / (1.0 + jnp.exp(-v)))


def _nt_dot(a, b):
    return lax.dot_general(a, b, (((1,), (1,)), ((), ())), preferred_element_type=F32)


def _ada_kernel(c_ref, w_ref, b_ref, o_ref):
    c_act = _silu(c_ref[...]).astype(BF16)
    w = w_ref[0].astype(BF16)
    o_ref[0] = jnp.dot(c_act, w, preferred_element_type=F32) + b_ref[0]


def _ada_modulation(c, w_ada, b_ada):
    tn = 768
    c_pad = jnp.pad(c, ((0, 8 - BATCH), (0, 0)))
    b3 = b_ada.reshape(DEPTH, 1, 3 * D_MODEL)
    return pl.pallas_call(
        _ada_kernel,
        out_shape=jax.ShapeDtypeStruct((DEPTH, 8, 3 * D_MODEL), F32),
        grid=(DEPTH, 3 * D_MODEL // tn),
        in_specs=[
            pl.BlockSpec((8, D_MODEL), lambda l, j: (0, 0)),
            pl.BlockSpec((1, D_MODEL, tn), lambda l, j: (l, 0, j)),
            pl.BlockSpec((1, 1, tn), lambda l, j: (l, 0, j)),
        ],
        out_specs=pl.BlockSpec((1, 8, tn), lambda l, j: (l, 0, j)),
        compiler_params=pltpu.CompilerParams(
            dimension_semantics=("arbitrary", "arbitrary"), vmem_limit_bytes=VMEM_LIMIT),
        name="ada_modulation",
    )(c_pad, w_ada, b3)


def _rope_kernel(pos_row_ref, pos_col_ref, invf_col_ref, invf_row_ref,
                 cos_t_ref, sin_t_ref, cos2_ref, sin2_ref):
    ang_t = invf_col_ref[...] * pos_row_ref[0].astype(F32)
    cos_t_ref[0] = jnp.cos(ang_t)
    sin_t_ref[0] = jnp.sin(ang_t)
    ang2 = pos_col_ref[0].astype(F32) * invf_row_ref[...]
    lane = lax.broadcasted_iota(jnp.int32, ang2.shape, 1)
    cos2_ref[0] = jnp.cos(ang2)
    sin2_ref[0] = jnp.where(lane < HALF_ROPE, -jnp.sin(ang2), jnp.sin(ang2))


def _rope_tables(positions):
    ts = 1024
    inv_freq = ROPE_THETA ** (-jnp.arange(0, QK_ROPE_DIM, 2, dtype=F32) / QK_ROPE_DIM)
    invf_col = inv_freq.reshape(HALF_ROPE, 1)
    invf_row = jnp.concatenate([inv_freq, inv_freq]).reshape(1, QK_ROPE_DIM)
    pos_row = positions.reshape(BATCH, 1, SEQ)
    pos_col = positions.reshape(BATCH, SEQ, 1)
    return pl.pallas_call(
        _rope_kernel,
        out_shape=(
            jax.ShapeDtypeStruct((BATCH, HALF_ROPE, SEQ), F32),
            jax.ShapeDtypeStruct((BATCH, HALF_ROPE, SEQ), F32),
            jax.ShapeDtypeStruct((BATCH, SEQ, QK_ROPE_DIM), F32),
            jax.ShapeDtypeStruct((BATCH, SEQ, QK_ROPE_DIM), F32),
        ),
        grid=(BATCH, SEQ // ts),
        in_specs=[
            pl.BlockSpec((1, 1, ts), lambda b, s: (b, 0, s)),
            pl.BlockSpec((1, ts, 1), lambda b, s: (b, s, 0)),
            pl.BlockSpec((HALF_ROPE, 1), lambda b, s: (0, 0)),
            pl.BlockSpec((1, QK_ROPE_DIM), lambda b, s: (0, 0)),
        ],
        out_specs=(
            pl.BlockSpec((1, HALF_ROPE, ts), lambda b, s: (b, 0, s)),
            pl.BlockSpec((1, HALF_ROPE, ts), lambda b, s: (b, 0, s)),
            pl.BlockSpec((1, ts, QK_ROPE_DIM), lambda b, s: (b, s, 0)),
            pl.BlockSpec((1, ts, QK_ROPE_DIM), lambda b, s: (b, s, 0)),
        ),
        compiler_params=pltpu.CompilerParams(
            dimension_semantics=("arbitrary", "arbitrary"), vmem_limit_bytes=VMEM_LIMIT),
        name="rope_tables",
    )(pos_row, pos_col, invf_col, invf_row)


def _channel_dft_matrix():
    idx = np.arange(FOURIER_GROUP_DIM)
    ang = 2.0 * np.pi * ((idx[:, None] * idx[None, :]) % FOURIER_GROUP_DIM) / FOURIER_GROUP_DIM
    scale = 1.0 / math.sqrt(FOURIER_GROUP_DIM)
    return np.concatenate([np.cos(ang), np.sin(ang)], axis=1).astype(np.float32) * np.float32(scale)


def _fold_kernel(cs_ref, w_ref, o_ref):
    w = w_ref[0]
    dims = (((1,), (0,)), ((), ()))
    a = lax.dot_general(cs_ref[:, :FOURIER_GROUP_DIM], w, dims,
                        precision=lax.Precision.HIGHEST, preferred_element_type=F32)
    b = lax.dot_general(cs_ref[:, FOURIER_GROUP_DIM:], w, dims,
                        precision=lax.Precision.HIGHEST, preferred_element_type=F32)
    o_ref[0] = jnp.concatenate([a, b], axis=1).astype(BF16)


def _fold_fourier_weights(w_fmix_l):
    g, c = FOURIER_GROUPS, FOURIER_GROUP_DIM
    return pl.pallas_call(
        _fold_kernel,
        out_shape=jax.ShapeDtypeStruct((g, c, 2 * c), BF16),
        grid=(g,),
        in_specs=[
            pl.BlockSpec((c, 2 * c), lambda i: (0, 0)),
            pl.BlockSpec((1, c, c), lambda i: (i, 0, 0)),
        ],
        out_specs=pl.BlockSpec((1, c, 2 * c), lambda i: (i, 0, 0)),
        compiler_params=pltpu.CompilerParams(dimension_semantics=("arbitrary",)),
        name="fold_fourier_weights",
    )(jnp.asarray(_channel_dft_matrix()), w_fmix_l)


def _token_kernel(x_ref, scale_ref, shift_ref, cos_t_ref, sin_t_ref, cos2_ref, sin2_ref,
                  w_in_ref, qn_ref, kvn_ref, wq_t_ref, wk_ref, wv_t_ref, ab_ref,
                  pq_ref, gf_ref, ga_ref, q_t_ref, k_ref, v_t_ref):
    x = x_ref[...]
    tm = x.shape[0]
    mu = jnp.mean(x, axis=-1, keepdims=True)
    xc = x - mu
    var = jnp.mean(xc * xc, axis=-1, keepdims=True)
    h = xc * lax.rsqrt(var + NORM_EPS) * (1.0 + scale_ref[0]) + shift_ref[0]
    hb = h.astype(BF16)

    def proj(lo, hi):
        return jnp.dot(hb, w_in_ref[0, :, lo:hi], preferred_element_type=F32)

    ub = proj(COL_U, COL_ZF).astype(BF16)
    c = FOURIER_GROUP_DIM
    gate_f = _silu(proj(COL_ZF, COL_CQ)).astype(BF16)
    for g in range(FOURIER_GROUPS):
        pq = jnp.dot(ub[:, g * c:(g + 1) * c], ab_ref[g], preferred_element_type=F32)
        pq_ref[0, 0, g] = pq[:, :c].astype(BF16)
        pq_ref[0, 1, g] = pq[:, c:].astype(BF16)
        gf_ref[0, g] = gate_f[:, g * c:(g + 1) * c]

    ga_ref[...] = _silu(proj(COL_ZA, W_IN_COLS)).astype(BF16)

    cq = proj(COL_CQ, COL_CKV)
    cqn = (cq * lax.rsqrt(jnp.mean(cq * cq, axis=-1, keepdims=True) + NORM_EPS)
           * qn_ref[...]).astype(BF16)
    q_t = _nt_dot(wq_t_ref[...], cqn) * Q_SCALE
    cos_t = cos_t_ref[0]
    sin_t = sin_t_ref[0]
    n_nope = N_HEADS * QK_NOPE_DIM
    n_half = N_HEADS * HALF_ROPE
    zeros_q = jnp.zeros((QK_PAD_DIM - QK_HEAD_DIM, tm), BF16)
    for hd in range(N_HEADS):
        q_t_ref[0, hd, 0:QK_NOPE_DIM, :] = (
            q_t[hd * QK_NOPE_DIM:(hd + 1) * QK_NOPE_DIM].astype(BF16))
        x1 = q_t[n_nope + hd * HALF_ROPE:n_nope + (hd + 1) * HALF_ROPE]
        x2 = q_t[n_nope + n_half + hd * HALF_ROPE:n_nope + n_half + (hd + 1) * HALF_ROPE]
        q_t_ref[0, hd, QK_NOPE_DIM:QK_NOPE_DIM + HALF_ROPE, :] = (
            x1 * cos_t - x2 * sin_t).astype(BF16)
        q_t_ref[0, hd, QK_NOPE_DIM + HALF_ROPE:QK_HEAD_DIM, :] = (
            x2 * cos_t + x1 * sin_t).astype(BF16)
        q_t_ref[0, hd, QK_HEAD_DIM:QK_PAD_DIM, :] = zeros_q

    ckv_kr = proj(COL_CKV, COL_ZA)
    ckv = ckv_kr[:, :KV_LORA_RANK]
    ckvn = (ckv * lax.rsqrt(jnp.mean(ckv * ckv, axis=-1, keepdims=True) + NORM_EPS)
            * kvn_ref[...]).astype(BF16)
    k_nope = jnp.dot(ckvn, wk_ref[...], preferred_element_type=F32)
    v_t = _nt_dot(wv_t_ref[...], ckvn)
    kr = ckv_kr[:, KV_LORA_RANK:KV_LORA_RANK + QK_ROPE_DIM]
    kr_swapped = jnp.concatenate([kr[:, HALF_ROPE:], kr[:, :HALF_ROPE]], axis=1)
    kr_rot = (kr * cos2_ref[0] + kr_swapped * sin2_ref[0]).astype(BF16)
    pad_lane = lax.broadcasted_iota(jnp.int32, (tm, QK_PAD_DIM - QK_HEAD_DIM), 1)
    kr_pad = jnp.concatenate(
        [kr_rot, jnp.where(pad_lane == 0, 1.0, 0.0).astype(BF16)], axis=1)
    for hd in range(N_HEADS):
        k_ref[0, hd, :, 0:QK_NOPE_DIM] = (
            k_nope[:, hd * QK_NOPE_DIM:(hd + 1) * QK_NOPE_DIM].astype(BF16))
        k_ref[0, hd, :, QK_NOPE_DIM:QK_PAD_DIM] = kr_pad
        v_t_ref[0, hd] = v_t[hd * V_HEAD_DIM:(hd + 1) * V_HEAD_DIM].astype(BF16)


def _resident(shape):
    zeros = (0,) * len(shape)
    return pl.BlockSpec(shape, lambda *_: zeros, pipeline_mode=pl.Buffered(1))


def _layer_resident(shape, layer):
    index = (layer,) + (0,) * len(shape)
    return pl.BlockSpec((1,) + tuple(shape), lambda *_: index, pipeline_mode=pl.Buffered(1))


def _token_stage(x2d, scale, shift, tables, w, w_in_all, layer):
    tm = 256
    nt = SEQ // tm
    cos_t, sin_t, cos2, sin2 = tables
    row = lambda i: (i, 0)
    per_batch = lambda i: (i // nt, 0, 0)
    bh = lambda i: (i // nt, 0, 0, i % nt)
    return pl.pallas_call(
        _token_kernel,
        out_shape=(
            jax.ShapeDtypeStruct((BATCH, 2, FOURIER_GROUPS, SEQ, FOURIER_GROUP_DIM), BF16),
            jax.ShapeDtypeStruct((BATCH, FOURIER_GROUPS, SEQ, FOURIER_GROUP_DIM), BF16),
            jax.ShapeDtypeStruct((TOKENS, D_ATTN), BF16),
            jax.ShapeDtypeStruct((BATCH, N_HEADS, QK_PAD_DIM, SEQ), BF16),
            jax.ShapeDtypeStruct((BATCH, N_HEADS, SEQ, QK_PAD_DIM), BF16),
            jax.ShapeDtypeStruct((BATCH, N_HEADS, V_HEAD_DIM, SEQ), BF16),
        ),
        grid=(TOKENS // tm,),
        in_specs=[
            pl.BlockSpec((tm, D_MODEL), row),
            pl.BlockSpec((1, 1, D_MODEL), per_batch),
            pl.BlockSpec((1, 1, D_MODEL), per_batch),
            pl.BlockSpec((1, HALF_ROPE, tm), lambda i: (i // nt, 0, i % nt)),
            pl.BlockSpec((1, HALF_ROPE, tm), lambda i: (i // nt, 0, i % nt)),
            pl.BlockSpec((1, tm, QK_ROPE_DIM), lambda i: (i // nt, i % nt, 0)),
            pl.BlockSpec((1, tm, QK_ROPE_DIM), lambda i: (i // nt, i % nt, 0)),
            _layer_resident((D_MODEL, W_IN_COLS), layer),
            _resident((1, Q_LORA_RANK)),
            _resident((1, KV_LORA_RANK)),
            _resident((N_HEADS * QK_HEAD_DIM, Q_LORA_RANK)),
            _resident((KV_LORA_RANK, N_HEADS * QK_NOPE_DIM)),
            _resident((N_HEADS * V_HEAD_DIM, KV_LORA_RANK)),
            _resident((FOURIER_GROUPS, FOURIER_GROUP_DIM, 2 * FOURIER_GROUP_DIM)),
        ],
        out_specs=(
            pl.BlockSpec((1, 2, FOURIER_GROUPS, tm, FOURIER_GROUP_DIM),
                         lambda i: (i // nt, 0, 0, i % nt, 0)),
            pl.BlockSpec((1, FOURIER_GROUPS, tm, FOURIER_GROUP_DIM),
                         lambda i: (i // nt, 0, i % nt, 0)),
            pl.BlockSpec((tm, D_ATTN), row),
            pl.BlockSpec((1, N_HEADS, QK_PAD_DIM, tm), bh),
            pl.BlockSpec((1, N_HEADS, tm, QK_PAD_DIM), lambda i: (i // nt, 0, i % nt, 0)),
            pl.BlockSpec((1, N_HEADS, V_HEAD_DIM, tm), bh),
        ),
        compiler_params=pltpu.CompilerParams(
            dimension_semantics=("arbitrary",), vmem_limit_bytes=VMEM_LIMIT),
        name="token_stage",
    )(x2d, scale, shift, cos_t, sin_t, cos2, sin2,
      w_in_all, w["q_norm"], w["kv_norm"], w["wq_t"], w["wk"], w["wv_t"], w["ab"])


DFT_LEVELS = 3
DFT_BLOCKS = 2 ** DFT_LEVELS
DFT_SUB = SEQ // DFT_BLOCKS
DFT_GROUPS_PER_STEP = 2
DFT_PASSES = ((0, 3),)


def _dft_constants():
    tw = []
    for level in range(DFT_LEVELS):
        length = SEQ >> level
        ang = 2.0 * np.pi * np.arange(length // 2) / length
        for f in (np.cos, np.sin):
            col = jnp.asarray(f(ang).astype(np.float32)).reshape(-1, 1)
            tw.append(jnp.broadcast_to(col, (length // 2, FOURIER_GROUP_DIM)))
    idx = np.arange(DFT_SUB)
    ang = 2.0 * np.pi * ((idx[:, None] * idx[None, :]) % DFT_SUB) / DFT_SUB
    mat = np.concatenate([np.cos(ang), -np.sin(ang)], axis=1) / math.sqrt(SEQ)
    return tw, jnp.asarray(mat.astype(np.float32)).astype(BF16)


def _bit_reverse(j, bits):
    return int(format(j, "0{}b".format(bits))[::-1], 2)


def _dft_kernel(*refs):
    tw_refs = refs[:2 * DFT_LEVELS]
    m_ref, pq_ref, g_ref, o_ref, p_scr, w_scr = refs[2 * DFT_LEVELS:2 * DFT_LEVELS + 6]
    y_scr = refs[2 * DFT_LEVELS + 6:]
    gps, c = DFT_GROUPS_PER_STEP, FOURIER_GROUP_DIM

    for first_level, n_levels in DFT_PASSES:
        from_input = first_level == 0
        rc = 16 if from_input else 8
        n_blocks = 2 ** n_levels
        blk = SEQ >> (first_level + n_levels)

        def body(i, carry, first_level=first_level, n_levels=n_levels, from_input=from_input,
                 rc=rc, n_blocks=n_blocks, blk=blk):
            r = pl.multiple_of(i * rc, rc)
            tw = {}
            for t in range(n_levels):
                for j in range(n_blocks >> (t + 1)):
                    rows = pl.ds(j * blk + r, rc)
                    tw[t, j] = (tw_refs[2 * (first_level + t)][rows, :],
                                tw_refs[2 * (first_level + t) + 1][rows, :])
            for sub in range(2 ** first_level):
                base = sub * n_blocks * blk
                for g in range(gps):
                    lanes = slice(g * c, (g + 1) * c)
                    data = []
                    for j in range(n_blocks):
                        rows = pl.ds(base + j * blk + r, rc)
                        if from_input:
                            data.append((pq_ref[0, 0, g, rows, :].astype(F32),
                                         pq_ref[0, 1, g, rows, :].astype(F32)))
                        else:
                            data.append((p_scr[rows, lanes], w_scr[rows, lanes]))
                    for t in range(n_levels):
                        half = n_blocks >> (t + 1)
                        for g0 in range(0, n_blocks, 2 * half):
                            for j in range(half):
                                (pa, wa), (pb, wb) = data[g0 + j], data[g0 + j + half]
                                cos, sin = tw[t, j]
                                dp, dw = pa - pb, wa - wb
                                data[g0 + j] = (pa + pb, wa + wb)
                                data[g0 + j + half] = (dp * cos - dw * sin, dw * cos + dp * sin)
                    for j in range(n_blocks):
                        rows = pl.ds(base + j * blk + r, rc)
                        p_scr[rows, lanes] = data[j][0]
                        w_scr[rows, lanes] = data[j][1]
            return carry

        lax.fori_loop(0, blk // rc, body, 0)

    for j in range(DFT_BLOCKS):
        rows = slice(j * DFT_SUB, (j + 1) * DFT_SUB)
        z = jnp.concatenate([p_scr[rows, :], w_scr[rows, :]], axis=0).astype(BF16)
        y = jnp.dot(m_ref[...], z, preferred_element_type=F32)
        out_rows = pl.ds(_bit_reverse(j, DFT_LEVELS), DFT_SUB, stride=DFT_BLOCKS)
        for g in range(gps):
            y_scr[g][out_rows, :] = y[:, g * c:(g + 1) * c]
    for g in range(gps):
        o_ref[0, g] = (y_scr[g][...] * g_ref[0, g].astype(F32)).astype(BF16)


def _seq_dft(pq, gate_f, consts):
    tw, mat = consts
    gps, c = DFT_GROUPS_PER_STEP, FOURIER_GROUP_DIM
    n_gsteps = FOURIER_GROUPS // gps
    return pl.pallas_call(
        _dft_kernel,
        out_shape=jax.ShapeDtypeStruct((BATCH, FOURIER_GROUPS, SEQ, c), BF16),
        grid=(BATCH, n_gsteps),
        in_specs=(
            [_resident(t.shape) for t in tw]
            + [_resident(mat.shape),
               pl.BlockSpec((1, 2, gps, SEQ, c), lambda b, gi: (b, 0, gi, 0, 0)),
               pl.BlockSpec((1, gps, SEQ, c), lambda b, gi: (b, gi, 0, 0))]),
        out_specs=pl.BlockSpec((1, gps, SEQ, c), lambda b, gi: (b, gi, 0, 0)),
        scratch_shapes=(
            [pltpu.VMEM((SEQ, gps * c), F32)] * 2 + [pltpu.VMEM((SEQ, c), F32)] * gps),
        compiler_params=pltpu.CompilerParams(
            dimension_semantics=("arbitrary", "arbitrary"), vmem_limit_bytes=VMEM_LIMIT),
        name="seq_dft",
    )(*tw, mat, pq, gate_f)


ATTN_PROBE_KEYS = 128
ATTN_MAX_DENOM = 2.0 ** 16


def _attn_kernel(q_t_ref, k_ref, v_t_ref, g_ref, o_ref, q_aug, s_scr, p_scr, *, tk):
    tq = q_t_ref.shape[3]
    n_chunks = SEQ // tk

    def finish(acc, l):
        o = (acc * (1.0 / l)).T
        o_ref[...] = (o * g_ref[...].astype(F32)).astype(BF16)

    s_probe = jnp.dot(k_ref[0, 0, 0:ATTN_PROBE_KEYS, :], q_t_ref[0, 0], preferred_element_type=F32)
    stab = jnp.max(s_probe, axis=0, keepdims=True)
    q_aug[...] = q_t_ref[0, 0]
    pad_rows = QK_PAD_DIM - QK_HEAD_DIM
    first = lax.broadcasted_iota(jnp.int32, (pad_rows, tq), 0) == 0
    q_aug[QK_HEAD_DIM:QK_PAD_DIM, :] = jnp.where(first, -stab, 0.0).astype(BF16)
    def scores(c):
        s_scr[c % 2] = jnp.dot(k_ref[0, 0, c * tk:(c + 1) * tk, :], q_aug[...],
                               preferred_element_type=F32)

    def probs(c):
        p = jnp.exp2(s_scr[c % 2])
        p_scr[c % 2] = p.astype(BF16)
        return jnp.sum(p, axis=0, keepdims=True)

    def values(c):
        return jnp.dot(v_t_ref[0, 0, :, c * tk:(c + 1) * tk], p_scr[c % 2],
                       preferred_element_type=F32)

    scores(0)
    scores(1)
    l = probs(0)
    acc = jnp.zeros((V_HEAD_DIM, tq), F32)
    for c in range(1, n_chunks):
        if c + 1 < n_chunks:
            scores(c + 1)
        l = l + probs(c)
        acc = acc + values(c - 1)
    acc = acc + values(n_chunks - 1)
    finish(acc, l)

    @pl.when(jnp.logical_not(jnp.max(l) <= ATTN_MAX_DENOM))
    def _():
        m = jnp.full((1, tq), -jnp.inf, F32)
        l2 = jnp.zeros((1, tq), F32)
        acc2 = jnp.zeros((V_HEAD_DIM, tq), F32)
        for c in range(n_chunks):
            s = jnp.dot(k_ref[0, 0, c * tk:(c + 1) * tk, :], q_t_ref[0, 0],
                        preferred_element_type=F32)
            m_new = jnp.maximum(m, jnp.max(s, axis=0, keepdims=True))
            alpha = jnp.exp2(m - m_new)
            p = jnp.exp2(s - m_new)
            l2 = alpha * l2 + jnp.sum(p, axis=0, keepdims=True)
            acc2 = alpha * acc2 + jnp.dot(v_t_ref[0, 0, :, c * tk:(c + 1) * tk], p.astype(BF16),
                                          preferred_element_type=F32)
            m = m_new
        finish(acc2, l2)


def _attention(q_t, k, v_t, gate_a):
    tq, tk = 1024, 512
    nq = SEQ // tq
    return pl.pallas_call(
        functools.partial(_attn_kernel, tk=tk),
        scratch_shapes=[
            pltpu.VMEM((QK_PAD_DIM, tq), BF16),
            pltpu.VMEM((2, tk, tq), F32),
            pltpu.VMEM((2, tk, tq), BF16),
        ],
        out_shape=jax.ShapeDtypeStruct((TOKENS, D_ATTN), BF16),
        grid=(BATCH, N_HEADS, nq),
        in_specs=[
            pl.BlockSpec((1, 1, QK_PAD_DIM, tq), lambda b, h, i: (b, h, 0, i)),
            pl.BlockSpec((1, 1, SEQ, QK_PAD_DIM), lambda b, h, i: (b, h, 0, 0)),
            pl.BlockSpec((1, 1, V_HEAD_DIM, SEQ), lambda b, h, i: (b, h, 0, 0)),
            pl.BlockSpec((tq, V_HEAD_DIM), lambda b, h, i: (b * nq + i, h)),
        ],
        out_specs=pl.BlockSpec((tq, V_HEAD_DIM), lambda b, h, i: (b * nq + i, h)),
        compiler_params=pltpu.CompilerParams(
            dimension_semantics=("arbitrary", "arbitrary", "arbitrary"),
            vmem_limit_bytes=VMEM_LIMIT),
        name="attention",
    )(q_t, k, v_t, gate_a)


OUT_SUB_ROWS = 256


def _out_kernel(yf_ref, ya_ref, x_ref, gate_ref, w_ref, g_ref, b_ref, o_ref):
    for rows in (slice(i * OUT_SUB_ROWS, (i + 1) * OUT_SUB_ROWS)
                 for i in range(x_ref.shape[0] // OUT_SUB_ROWS)):
        y_f = jnp.concatenate([yf_ref[0, g, rows, :] for g in range(FOURIER_GROUPS)], axis=1)
        y = (jnp.dot(y_f, w_ref[0, :D_FOURIER, :], preferred_element_type=F32)
             + jnp.dot(ya_ref[rows, :], w_ref[0, D_FOURIER:, :], preferred_element_type=F32))
        r = DEEPNORM_ALPHA * x_ref[rows, :] + gate_ref[0] * y
        mu = jnp.mean(r, axis=-1, keepdims=True)
        rc = r - mu
        var = jnp.mean(rc * rc, axis=-1, keepdims=True)
        o_ref[rows, :] = rc * lax.rsqrt(var + NORM_EPS) * g_ref[...] + b_ref[...]


def _output_stage(y_f, y_a, x2d, gate, w_out_bf16, ln_g, ln_b, layer):
    tm = 512
    nt = SEQ // tm
    row = lambda i: (i, 0)
    return pl.pallas_call(
        _out_kernel,
        out_shape=jax.ShapeDtypeStruct((TOKENS, D_MODEL), F32),
        grid=(TOKENS // tm,),
        in_specs=[
            pl.BlockSpec((1, FOURIER_GROUPS, tm, FOURIER_GROUP_DIM),
                         lambda i: (i // nt, 0, i % nt, 0)),
            pl.BlockSpec((tm, D_ATTN), row),
            pl.BlockSpec((tm, D_MODEL), row),
            pl.BlockSpec((1, 1, D_MODEL), lambda i: (i // nt, 0, 0)),
            _layer_resident((D_MODEL, D_MODEL), layer),
            _resident((1, D_MODEL)),
            _resident((1, D_MODEL)),
        ],
        out_specs=pl.BlockSpec((tm, D_MODEL), row),
        compiler_params=pltpu.CompilerParams(
            dimension_semantics=("arbitrary",), vmem_limit_bytes=VMEM_LIMIT),
        name="output_stage",
    )(y_f, y_a, x2d, gate, w_out_bf16, ln_g, ln_b)


def _w_in_kernel(wt_ref, o_ref):
    pad = W_IN_COLS - wt_ref.shape[1]
    split = COL_ZA - pad
    aligned = COL_ZA - 128
    o_ref[0, :, 0:aligned] = wt_ref[0, 0:aligned, :].T.astype(BF16)
    mid = wt_ref[0, aligned:aligned + 128, :].T
    lane = lax.broadcasted_iota(jnp.int32, mid.shape, 1)
    o_ref[0, :, aligned:COL_ZA] = jnp.where(lane < split - aligned, mid, 0.0).astype(BF16)
    o_ref[0, :, COL_ZA:W_IN_COLS] = wt_ref[0, split:, :].T.astype(BF16)


def _prepare_w_in(w_in):
    rows = 512
    cols = w_in.shape[2]
    return pl.pallas_call(
        _w_in_kernel,
        out_shape=jax.ShapeDtypeStruct((DEPTH, D_MODEL, W_IN_COLS), BF16),
        grid=(DEPTH, D_MODEL // rows),
        in_specs=[pl.BlockSpec((1, cols, rows), lambda l, i: (l, 0, i))],
        out_specs=pl.BlockSpec((1, rows, W_IN_COLS), lambda l, i: (l, i, 0)),
        compiler_params=pltpu.CompilerParams(
            dimension_semantics=("arbitrary", "arbitrary"), vmem_limit_bytes=VMEM_LIMIT),
        name="prepare_w_in",
    )(jnp.swapaxes(w_in, 1, 2))


def _layer_weights(q_norm_l, w_q_b_l, kv_norm_l, w_kv_b_l, w_fmix_l):
    wq = w_q_b_l.reshape(Q_LORA_RANK, N_HEADS, QK_HEAD_DIM)
    wq = jnp.concatenate([
        wq[:, :, :QK_NOPE_DIM].reshape(Q_LORA_RANK, -1),
        wq[:, :, QK_NOPE_DIM:QK_NOPE_DIM + HALF_ROPE].reshape(Q_LORA_RANK, -1),
        wq[:, :, QK_NOPE_DIM + HALF_ROPE:].reshape(Q_LORA_RANK, -1)], axis=1)
    wkv = w_kv_b_l.reshape(KV_LORA_RANK, N_HEADS, QK_NOPE_DIM + V_HEAD_DIM)
    wk = wkv[:, :, :QK_NOPE_DIM].reshape(KV_LORA_RANK, -1)
    wv = wkv[:, :, QK_NOPE_DIM:].reshape(KV_LORA_RANK, -1)
    return {
        "q_norm": q_norm_l.reshape(1, Q_LORA_RANK),
        "kv_norm": kv_norm_l.reshape(1, KV_LORA_RANK),
        "wq_t": wq.T.astype(BF16),
        "wk": wk.astype(BF16),
        "wv_t": wv.T.astype(BF16),
        "ab": _fold_fourier_weights(w_fmix_l),
    }


def kernel(x, c, positions, w_ada, b_ada, w_in, q_norm, w_q_b, kv_norm, w_kv_b, w_fmix, w_out,
           ln_g, ln_b):
    assert x.shape == (BATCH, SEQ, D_MODEL) and w_ada.shape[0] == DEPTH
    mod = _ada_modulation(c, w_ada, b_ada)
    tables = _rope_tables(positions)
    dft_consts = _dft_constants()
    w_in = _prepare_w_in(w_in)
    w_out_bf16 = w_out.astype(BF16)
    x2d = x.reshape(TOKENS, D_MODEL)
    for l in range(DEPTH):
        shift, scale, gate = (
            mod[l, :BATCH, i * D_MODEL:(i + 1) * D_MODEL].reshape(BATCH, 1, D_MODEL)
            for i in range(3))
        w = _layer_weights(q_norm[l], w_q_b[l], kv_norm[l], w_kv_b[l], w_fmix[l])
        pq, gate_f, gate_a, q_t, k, v_t = _token_stage(x2d, scale, shift, tables, w, w_in, l)
        y_f = _seq_dft(pq, gate_f, dft_consts)
        y_a = _attention(q_t, k, v_t, gate_a)
        x2d = _output_stage(y_f, y_a, x2d, gate, w_out_bf16,
                            ln_g[l].reshape(1, D_MODEL), ln_b[l].reshape(1, D_MODEL), l)
    return x2d.reshape(BATCH, SEQ, D_MODEL)
```

```python
import functools
import math

import numpy as np
import jax
import jax.numpy as jnp
from jax import lax
from jax.experimental import pallas as pl
from jax.experimental.pallas import tpu as pltpu

D_MODEL = 2048
BATCH = 4
SEQ = 4096
DEPTH = 2
D_FOURIER = D_MODEL // 2
FOURIER_GROUPS = 8
FOURIER_GROUP_DIM = D_FOURIER // FOURIER_GROUPS
V_HEAD_DIM = 128
D_ATTN = D_MODEL // 2
N_HEADS = D_ATTN // V_HEAD_DIM
QK_NOPE_DIM = 128
QK_ROPE_DIM = 64
QK_HEAD_DIM = QK_NOPE_DIM + QK_ROPE_DIM
Q_LORA_RANK = D_MODEL // 4
KV_LORA_RANK = D_MODEL // 8
ROPE_THETA = 10000.0
NORM_EPS = 1e-6
DEEPNORM_ALPHA = (2 * DEPTH) ** 0.25

TOKENS = BATCH * SEQ
HALF_ROPE = QK_ROPE_DIM // 2
QK_PAD_DIM = 256
W_IN_COLS = 3968
COL_U, COL_ZF, COL_CQ, COL_CKV, COL_ZA = 0, 1024, 2048, 2560, 2944

F32 = jnp.float32
BF16 = jnp.bfloat16
VMEM_LIMIT = 52 * 1024 * 1024

Q_SCALE = math.log2(math.e) / math.sqrt(QK_HEAD_DIM)


def _silu(v):
    return v * (1.0 / (1.0 + jnp.exp(-v)))


def _nt_dot(a, b):
    return lax.dot_general(a, b, (((1,), (1,)), ((), ())), preferred_element_type=F32)


def _ada_kernel(c_ref, w_ref, b_ref, o_ref):
    c_act = _silu(c_ref[...]).astype(BF16)
    w = w_ref[0].astype(BF16)
    o_ref[0] = jnp.dot(c_act, w, preferred_element_type=F32) + b_ref[0]


def _ada_modulation(c, w_ada, b_ada):
    tn = 768
    c_pad = jnp.pad(c, ((0, 8 - BATCH), (0, 0)))
    b3 = b_ada.reshape(DEPTH, 1, 3 * D_MODEL)
    return pl.pallas_call(
        _ada_kernel,
        out_shape=jax.ShapeDtypeStruct((DEPTH, 8, 3 * D_MODEL), F32),
        grid=(DEPTH, 3 * D_MODEL // tn),
        in_specs=[
            pl.BlockSpec((8, D_MODEL), lambda l, j: (0, 0)),
            pl.BlockSpec((1, D_MODEL, tn), lambda l, j: (l, 0, j)),
            pl.BlockSpec((1, 1, tn), lambda l, j: (l, 0, j)),
        ],
        out_specs=pl.BlockSpec((1, 8, tn), lambda l, j: (l, 0, j)),
        compiler_params=pltpu.CompilerParams(
            dimension_semantics=("arbitrary", "arbitrary"), vmem_limit_bytes=VMEM_LIMIT),
        name="ada_modulation",
    )(c_pad, w_ada, b3)


def _rope_kernel(pos_row_ref, pos_col_ref, invf_col_ref, invf_row_ref,
                 cos_t_ref, sin_t_ref, cos2_ref, sin2_ref):
    ang_t = invf_col_ref[...] * pos_row_ref[0].astype(F32)
    cos_t_ref[0] = jnp.cos(ang_t)
    sin_t_ref[0] = jnp.sin(ang_t)
    ang2 = pos_col_ref[0].astype(F32) * invf_row_ref[...]
    lane = lax.broadcasted_iota(jnp.int32, ang2.shape, 1)
    cos2_ref[0] = jnp.cos(ang2)
    sin2_ref[0] = jnp.where(lane < HALF_ROPE, -jnp.sin(ang2), jnp.sin(ang2))


def _rope_tables(positions):
    ts = 1024
    inv_freq = ROPE_THETA ** (-jnp.arange(0, QK_ROPE_DIM, 2, dtype=F32) / QK_ROPE_DIM)
    invf_col = inv_freq.reshape(HALF_ROPE, 1)
    invf_row = jnp.concatenate([inv_freq, inv_freq]).reshape(1, QK_ROPE_DIM)
    pos_row = positions.reshape(BATCH, 1, SEQ)
    pos_col = positions.reshape(BATCH, SEQ, 1)
    return pl.pallas_call(
        _rope_kernel,
        out_shape=(
            jax.ShapeDtypeStruct((BATCH, HALF_ROPE, SEQ), F32),
            jax.ShapeDtypeStruct((BATCH, HALF_ROPE, SEQ), F32),
            jax.ShapeDtypeStruct((BATCH, SEQ, QK_ROPE_DIM), F32),
            jax.ShapeDtypeStruct((BATCH, SEQ, QK_ROPE_DIM), F32),
        ),
        grid=(BATCH, SEQ // ts),
        in_specs=[
            pl.BlockSpec((1, 1, ts), lambda b, s: (b, 0, s)),
            pl.BlockSpec((1, ts, 1), lambda b, s: (b, s, 0)),
            pl.BlockSpec((HALF_ROPE, 1), lambda b, s: (0, 0)),
            pl.BlockSpec((1, QK_ROPE_DIM), lambda b, s: (0, 0)),
        ],
        out_specs=(
            pl.BlockSpec((1, HALF_ROPE, ts), lambda b, s: (b, 0, s)),
            pl.BlockSpec((1, HALF_ROPE, ts), lambda b, s: (b, 0, s)),
            pl.BlockSpec((1, ts, QK_ROPE_DIM), lambda b, s: (b, s, 0)),
            pl.BlockSpec((1, ts, QK_ROPE_DIM), lambda b, s: (b, s, 0)),
        ),
        compiler_params=pltpu.CompilerParams(
            dimension_semantics=("arbitrary", "arbitrary"), vmem_limit_bytes=VMEM_LIMIT),
        name="rope_tables",
    )(pos_row, pos_col, invf_col, invf_row)


def _channel_dft_matrix():
    idx = np.arange(FOURIER_GROUP_DIM)
    ang = 2.0 * np.pi * ((idx[:, None] * idx[None, :]) % FOURIER_GROUP_DIM) / FOURIER_GROUP_DIM
    scale = 1.0 / math.sqrt(FOURIER_GROUP_DIM)
    return np.concatenate([np.cos(ang), np.sin(ang)], axis=1).astype(np.float32) * np.float32(scale)


def _fold_kernel(cs_ref, w_ref, o_ref):
    w = w_ref[0]
    dims = (((1,), (0,)), ((), ()))
    a = lax.dot_general(cs_ref[:, :FOURIER_GROUP_DIM], w, dims,
                        precision=lax.Precision.HIGHEST, preferred_element_type=F32)
    b = lax.dot_general(cs_ref[:, FOURIER_GROUP_DIM:], w, dims,
                        precision=lax.Precision.HIGHEST, preferred_element_type=F32)
    o_ref[0] = jnp.concatenate([a, b], axis=1).astype(BF16)


def _fold_fourier_weights(w_fmix_l):
    g, c = FOURIER_GROUPS, FOURIER_GROUP_DIM
    return pl.pallas_call(
        _fold_kernel,
        out_shape=jax.ShapeDtypeStruct((g, c, 2 * c), BF16),
        grid=(g,),
        in_specs=[
            pl.BlockSpec((c, 2 * c), lambda i: (0, 0)),
            pl.BlockSpec((1, c, c), lambda i: (i, 0, 0)),
        ],
        out_specs=pl.BlockSpec((1, c, 2 * c), lambda i: (i, 0, 0)),
        compiler_params=pltpu.CompilerParams(dimension_semantics=("arbitrary",)),
        name="fold_fourier_weights",
    )(jnp.asarray(_channel_dft_matrix()), w_fmix_l)


def _token_kernel(x_ref, scale_ref, shift_ref, cos_t_ref, sin_t_ref, cos2_ref, sin2_ref,
                  w_in_ref, qn_ref, kvn_ref, wq_t_ref, wk_ref, wv_t_ref, ab_ref,
                  pq_ref, gf_ref, ga_ref, q_t_ref, k_ref, v_t_ref):
    x = x_ref[...]
    tm = x.shape[0]
    mu = jnp.mean(x, axis=-1, keepdims=True)
    xc = x - mu
    var = jnp.mean(xc * xc, axis=-1, keepdims=True)
    h = xc * lax.rsqrt(var + NORM_EPS) * (1.0 + scale_ref[0]) + shift_ref[0]
    hb = h.astype(BF16)

    def proj(lo, hi):
        return jnp.dot(hb, w_in_ref[0, :, lo:hi], preferred_element_type=F32)

    ub = proj(COL_U, COL_ZF).astype(BF16)
    c = FOURIER_GROUP_DIM
    gate_f = _silu(proj(COL_ZF, COL_CQ)).astype(BF16)
    for g in range(FOURIER_GROUPS):
        pq = jnp.dot(ub[:, g * c:(g + 1) * c], ab_ref[g], preferred_element_type=F32)
        pq_ref[0, 0, g] = pq[:, :c].astype(BF16)
        pq_ref[0, 1, g] = pq[:, c:].astype(BF16)
        gf_ref[0, g] = gate_f[:, g * c:(g + 1) * c]

    ga_ref[...] = _silu(proj(COL_ZA, W_IN_COLS)).astype(BF16)

    cq = proj(COL_CQ, COL_CKV)
    cqn = (cq * lax.rsqrt(jnp.mean(cq * cq, axis=-1, keepdims=True) + NORM_EPS)
           * qn_ref[...]).astype(BF16)
    q_t = _nt_dot(wq_t_ref[...], cqn) * Q_SCALE
    cos_t = cos_t_ref[0]
    sin_t = sin_t_ref[0]
    n_nope = N_HEADS * QK_NOPE_DIM
    n_half = N_HEADS * HALF_ROPE
    zeros_q = jnp.zeros((QK_PAD_DIM - QK_HEAD_DIM, tm), BF16)
    for hd in range(N_HEADS):
        q_t_ref[0, hd, 0:QK_NOPE_DIM, :] = (
            q_t[hd * QK_NOPE_DIM:(hd + 1) * QK_NOPE_DIM].astype(BF16))
        x1 = q_t[n_nope + hd * HALF_ROPE:n_nope + (hd + 1) * HALF_ROPE]
        x2 = q_t[n_nope + n_half + hd * HALF_ROPE:n_nope + n_half + (hd + 1) * HALF_ROPE]
        q_t_ref[0, hd, QK_NOPE_DIM:QK_NOPE_DIM + HALF_ROPE, :] = (
            x1 * cos_t - x2 * sin_t).astype(BF16)
        q_t_ref[0, hd, QK_NOPE_DIM + HALF_ROPE:QK_HEAD_DIM, :] = (
            x2 * cos_t + x1 * sin_t).astype(BF16)
        q_t_ref[0, hd, QK_HEAD_DIM:QK_PAD_DIM, :] = zeros_q

    ckv_kr = proj(COL_CKV, COL_ZA)
    ckv = ckv_kr[:, :KV_LORA_RANK]
    ckvn = (ckv * lax.rsqrt(jnp.mean(ckv * ckv, axis=-1, keepdims=True) + NORM_EPS)
            * kvn_ref[...]).astype(BF16)
    k_nope = jnp.dot(ckvn, wk_ref[...], preferred_element_type=F32)
    v_t = _nt_dot(wv_t_ref[...], ckvn)
    kr = ckv_kr[:, KV_LORA_RANK:KV_LORA_RANK + QK_ROPE_DIM]
    kr_swapped = jnp.concatenate([kr[:, HALF_ROPE:], kr[:, :HALF_ROPE]], axis=1)
    kr_rot = (kr * cos2_ref[0] + kr_swapped * sin2_ref[0]).astype(BF16)
    pad_lane = lax.broadcasted_iota(jnp.int32, (tm, QK_PAD_DIM - QK_HEAD_DIM), 1)
    kr_pad = jnp.concatenate(
        [kr_rot, jnp.where(pad_lane == 0, 1.0, 0.0).astype(BF16)], axis=1)
    for hd in range(N_HEADS):
        k_ref[0, hd, :, 0:QK_NOPE_DIM] = (
            k_nope[:, hd * QK_NOPE_DIM:(hd + 1) * QK_NOPE_DIM].astype(BF16))
        k_ref[0, hd, :, QK_NOPE_DIM:QK_PAD_DIM] = kr_pad
        v_t_ref[0, hd] = v_t[hd * V_HEAD_DIM:(hd + 1) * V_HEAD_DIM].astype(BF16)


def _resident(shape):
    zeros = (0,) * len(shape)
    return pl.BlockSpec(shape, lambda *_: zeros, pipeline_mode=pl.Buffered(1))


def _layer_resident(shape, layer):
    index = (layer,) + (0,) * len(shape)
    return pl.BlockSpec((1,) + tuple(shape), lambda *_: index, pipeline_mode=pl.Buffered(1))


def _token_stage(x2d, scale, shift, tables, w, w_in_all, layer):
    tm = 512
    nt = SEQ // tm
    cos_t, sin_t, cos2, sin2 = tables
    row = lambda i: (i, 0)
    per_batch = lambda i: (i // nt, 0, 0)
    bh = lambda i: (i // nt, 0, 0, i % nt)
    return pl.pallas_call(
        _token_kernel,
        out_shape=(
            jax.ShapeDtypeStruct((BATCH, 2, FOURIER_GROUPS, SEQ, FOURIER_GROUP_DIM), BF16),
            jax.ShapeDtypeStruct((BATCH, FOURIER_GROUPS, SEQ, FOURIER_GROUP_DIM), BF16),
            jax.ShapeDtypeStruct((TOKENS, D_ATTN), BF16),
            jax.ShapeDtypeStruct((BATCH, N_HEADS, QK_PAD_DIM, SEQ), BF16),
            jax.ShapeDtypeStruct((BATCH, N_HEADS, SEQ, QK_PAD_DIM), BF16),
            jax.ShapeDtypeStruct((BATCH, N_HEADS, V_HEAD_DIM, SEQ), BF16),
        ),
        grid=(TOKENS // tm,),
        in_specs=[
            pl.BlockSpec((tm, D_MODEL), row),
            pl.BlockSpec((1, 1, D_MODEL), per_batch),
            pl.BlockSpec((1, 1, D_MODEL), per_batch),
            pl.BlockSpec((1, HALF_ROPE, tm), lambda i: (i // nt, 0, i % nt)),
            pl.BlockSpec((1, HALF_ROPE, tm), lambda i: (i // nt, 0, i % nt)),
            pl.BlockSpec((1, tm, QK_ROPE_DIM), lambda i: (i // nt, i % nt, 0)),
            pl.BlockSpec((1, tm, QK_ROPE_DIM), lambda i: (i // nt, i % nt, 0)),
            _layer_resident((D_MODEL, W_IN_COLS), layer),
            _resident((1, Q_LORA_RANK)),
            _resident((1, KV_LORA_RANK)),
            _resident((N_HEADS * QK_HEAD_DIM, Q_LORA_RANK)),
            _resident((KV_LORA_RANK, N_HEADS * QK_NOPE_DIM)),
            _resident((N_HEADS * V_HEAD_DIM, KV_LORA_RANK)),
            _resident((FOURIER_GROUPS, FOURIER_GROUP_DIM, 2 * FOURIER_GROUP_DIM)),
        ],
        out_specs=(
            pl.BlockSpec((1, 2, FOURIER_GROUPS, tm, FOURIER_GROUP_DIM),
                         lambda i: (i // nt, 0, 0, i % nt, 0)),
            pl.BlockSpec((1, FOURIER_GROUPS, tm, FOURIER_GROUP_DIM),
                         lambda i: (i // nt, 0, i % nt, 0)),
            pl.BlockSpec((tm, D_ATTN), row),
            pl.BlockSpec((1, N_HEADS, QK_PAD_DIM, tm), bh),
            pl.BlockSpec((1, N_HEADS, tm, QK_PAD_DIM), lambda i: (i // nt, 0, i % nt, 0)),
            pl.BlockSpec((1, N_HEADS, V_HEAD_DIM, tm), bh),
        ),
        compiler_params=pltpu.CompilerParams(
            dimension_semantics=("arbitrary",), vmem_limit_bytes=VMEM_LIMIT),
        name="token_stage",
    )(x2d, scale, shift, cos_t, sin_t, cos2, sin2,
      w_in_all, w["q_norm"], w["kv_norm"], w["wq_t"], w["wk"], w["wv_t"], w["ab"])


DFT_LEVELS = 3
DFT_BLOCKS = 2 ** DFT_LEVELS
DFT_SUB = SEQ // DFT_BLOCKS
DFT_GROUPS_PER_STEP = 2
DFT_PASSES = ((0, 3),)


def _dft_constants():
    tw = []
    for level in range(DFT_LEVELS):
        length = SEQ >> level
        ang = 2.0 * np.pi * np.arange(length // 2) / length
        for f in (np.cos, np.sin):
            col = jnp.asarray(f(ang).astype(np.float32)).reshape(-1, 1)
            tw.append(jnp.broadcast_to(col, (length // 2, FOURIER_GROUP_DIM)))
    idx = np.arange(DFT_SUB)
    ang = 2.0 * np.pi * ((idx[:, None] * idx[None, :]) % DFT_SUB) / DFT_SUB
    mat = np.concatenate([np.cos(ang), -np.sin(ang)], axis=1) / math.sqrt(SEQ)
    return tw, jnp.asarray(mat.astype(np.float32)).astype(BF16)


def _bit_reverse(j, bits):
    return int(format(j, "0{}b".format(bits))[::-1], 2)


def _dft_kernel(*refs):
    tw_refs = refs[:2 * DFT_LEVELS]
    m_ref, pq_ref, g_ref, o_ref, p_scr, w_scr = refs[2 * DFT_LEVELS:2 * DFT_LEVELS + 6]
    y_scr = refs[2 * DFT_LEVELS + 6:]
    gps, c = DFT_GROUPS_PER_STEP, FOURIER_GROUP_DIM

    for first_level, n_levels in DFT_PASSES:
        from_input = first_level == 0
        rc = 16 if from_input else 8
        n_blocks = 2 ** n_levels
        blk = SEQ >> (first_level + n_levels)

        def body(i, carry, first_level=first_level, n_levels=n_levels, from_input=from_input,
                 rc=rc, n_blocks=n_blocks, blk=blk):
            r = pl.multiple_of(i * rc, rc)
            tw = {}
            for t in range(n_levels):
                for j in range(n_blocks >> (t + 1)):
                    rows = pl.ds(j * blk + r, rc)
                    tw[t, j] = (tw_refs[2 * (first_level + t)][rows, :],
                                tw_refs[2 * (first_level + t) + 1][rows, :])
            for sub in range(2 ** first_level):
                base = sub * n_blocks * blk
                for g in range(gps):
                    lanes = slice(g * c, (g + 1) * c)
                    data = []
                    for j in range(n_blocks):
                        rows = pl.ds(base + j * blk + r, rc)
                        if from_input:
                            data.append((pq_ref[0, 0, g, rows, :].astype(F32),
                                         pq_ref[0, 1, g, rows, :].astype(F32)))
                        else:
                            data.append((p_scr[rows, lanes], w_scr[rows, lanes]))
                    for t in range(n_levels):
                        half = n_blocks >> (t + 1)
                        for g0 in range(0, n_blocks, 2 * half):
                            for j in range(half):
                                (pa, wa), (pb, wb) = data[g0 + j], data[g0 + j + half]
                                cos, sin = tw[t, j]
                                dp, dw = pa - pb, wa - wb
                                data[g0 + j] = (pa + pb, wa + wb)
                                data[g0 + j + half] = (dp * cos - dw * sin, dw * cos + dp * sin)
                    for j in range(n_blocks):
                        rows = pl.ds(base + j * blk + r, rc)
                        p_scr[rows, lanes] = data[j][0]
                        w_scr[rows, lanes] = data[j][1]
            return carry

        lax.fori_loop(0, blk // rc, body, 0)

    for j in range(DFT_BLOCKS):
        rows = slice(j * DFT_SUB, (j + 1) * DFT_SUB)
        z = jnp.concatenate([p_scr[rows, :], w_scr[rows, :]], axis=0).astype(BF16)
        y = jnp.dot(m_ref[...], z, preferred_element_type=F32)
        out_rows = pl.ds(_bit_reverse(j, DFT_LEVELS), DFT_SUB, stride=DFT_BLOCKS)
        for g in range(gps):
            y_scr[g][out_rows, :] = y[:, g * c:(g + 1) * c]
    for g in range(gps):
        o_ref[0, g] = (y_scr[g][...] * g_ref[0, g].astype(F32)).astype(BF16)


def _seq_dft(pq, gate_f, consts):
    tw, mat = consts
    gps, c = DFT_GROUPS_PER_STEP, FOURIER_GROUP_DIM
    n_gsteps = FOURIER_GROUPS // gps
    return pl.pallas_call(
        _dft_kernel,
        out_shape=jax.ShapeDtypeStruct((BATCH, FOURIER_GROUPS, SEQ, c), BF16),
        grid=(BATCH, n_gsteps),
        in_specs=(
            [_resident(t.shape) for t in tw]
            + [_resident(mat.shape),
               pl.BlockSpec((1, 2, gps, SEQ, c), lambda b, gi: (b, 0, gi, 0, 0)),
               pl.BlockSpec((1, gps, SEQ, c), lambda b, gi: (b, gi, 0, 0))]),
        out_specs=pl.BlockSpec((1, gps, SEQ, c), lambda b, gi: (b, gi, 0, 0)),
        scratch_shapes=(
            [pltpu.VMEM((SEQ, gps * c), F32)] * 2 + [pltpu.VMEM((SEQ, c), F32)] * gps),
        compiler_params=pltpu.CompilerParams(
            dimension_semantics=("arbitrary", "arbitrary"), vmem_limit_bytes=VMEM_LIMIT),
        name="seq_dft",
    )(*tw, mat, pq, gate_f)


ATTN_PROBE_KEYS = 128
ATTN_MAX_DENOM = 2.0 ** 16


def _attn_kernel(q_t_ref, k_ref, v_t_ref, g_ref, o_ref, q_aug, s_scr, p_scr, *, tk):
    tq = q_t_ref.shape[3]
    n_chunks = SEQ // tk

    def finish(acc, l):
        o = (acc * (1.0 / l)).T
        o_ref[...] = (o * g_ref[...].astype(F32)).astype(BF16)

    s_probe = jnp.dot(k_ref[0, 0, 0:ATTN_PROBE_KEYS, :], q_t_ref[0, 0], preferred_element_type=F32)
    stab = jnp.max(s_probe, axis=0, keepdims=True)
    q_aug[...] = q_t_ref[0, 0]
    pad_rows = QK_PAD_DIM - QK_HEAD_DIM
    first = lax.broadcasted_iota(jnp.int32, (pad_rows, tq), 0) == 0
    q_aug[QK_HEAD_DIM:QK_PAD_DIM, :] = jnp.where(first, -stab, 0.0).astype(BF16)
    def scores(c):
        s_scr[c % 2] = jnp.dot(k_ref[0, 0, c * tk:(c + 1) * tk, :], q_aug[...],
                               preferred_element_type=F32)

    def probs(c):
        p = jnp.exp2(s_scr[c % 2])
        p_scr[c % 2] = p.astype(BF16)
        return jnp.sum(p, axis=0, keepdims=True)

    def values(c):
        return jnp.dot(v_t_ref[0, 0, :, c * tk:(c + 1) * tk], p_scr[c % 2],
                       preferred_element_type=F32)

    scores(0)
    scores(1)
    l = probs(0)
    acc = jnp.zeros((V_HEAD_DIM, tq), F32)
    for c in range(1, n_chunks):
        if c + 1 < n_chunks:
            scores(c + 1)
        l = l + probs(c)
        acc = acc + values(c - 1)
    acc = acc + values(n_chunks - 1)
    finish(acc, l)

    @pl.when(jnp.logical_not(jnp.max(l) <= ATTN_MAX_DENOM))
    def _():
        m = jnp.full((1, tq), -jnp.inf, F32)
        l2 = jnp.zeros((1, tq), F32)
        acc2 = jnp.zeros((V_HEAD_DIM, tq), F32)
        for c in range(n_chunks):
            s = jnp.dot(k_ref[0, 0, c * tk:(c + 1) * tk, :], q_t_ref[0, 0],
                        preferred_element_type=F32)
            m_new = jnp.maximum(m, jnp.max(s, axis=0, keepdims=True))
            alpha = jnp.exp2(m - m_new)
            p = jnp.exp2(s - m_new)
            l2 = alpha * l2 + jnp.sum(p, axis=0, keepdims=True)
            acc2 = alpha * acc2 + jnp.dot(v_t_ref[0, 0, :, c * tk:(c + 1) * tk], p.astype(BF16),
                                          preferred_element_type=F32)
            m = m_new
        finish(acc2, l2)


def _attention(q_t, k, v_t, gate_a):
    tq, tk = 2048, 512
    nq = SEQ // tq
    return pl.pallas_call(
        functools.partial(_attn_kernel, tk=tk),
        scratch_shapes=[
            pltpu.VMEM((QK_PAD_DIM, tq), BF16),
            pltpu.VMEM((2, tk, tq), F32),
            pltpu.VMEM((2, tk, tq), BF16),
        ],
        out_shape=jax.ShapeDtypeStruct((TOKENS, D_ATTN), BF16),
        grid=(BATCH, N_HEADS, nq),
        in_specs=[
            pl.BlockSpec((1, 1, QK_PAD_DIM, tq), lambda b, h, i: (b, h, 0, i)),
            pl.BlockSpec((1, 1, SEQ, QK_PAD_DIM), lambda b, h, i: (b, h, 0, 0)),
            pl.BlockSpec((1, 1, V_HEAD_DIM, SEQ), lambda b, h, i: (b, h, 0, 0)),
            pl.BlockSpec((tq, V_HEAD_DIM), lambda b, h, i: (b * nq + i, h)),
        ],
        out_specs=pl.BlockSpec((tq, V_HEAD_DIM), lambda b, h, i: (b * nq + i, h)),
        compiler_params=pltpu.CompilerParams(
            dimension_semantics=("arbitrary", "arbitrary", "arbitrary"),
            vmem_limit_bytes=VMEM_LIMIT),
        name="attention",
    )(q_t, k, v_t, gate_a)


OUT_SUB_ROWS = 256


def _out_kernel(yf_ref, ya_ref, x_ref, gate_ref, w_ref, g_ref, b_ref, o_ref):
    for rows in (slice(i * OUT_SUB_ROWS, (i + 1) * OUT_SUB_ROWS)
                 for i in range(x_ref.shape[0] // OUT_SUB_ROWS)):
        y_f = jnp.concatenate([yf_ref[0, g, rows, :] for g in range(FOURIER_GROUPS)], axis=1)
        y = (jnp.dot(y_f, w_ref[0, :D_FOURIER, :], preferred_element_type=F32)
             + jnp.dot(ya_ref[rows, :], w_ref[0, D_FOURIER:, :], preferred_element_type=F32))
        r = DEEPNORM_ALPHA * x_ref[rows, :] + gate_ref[0] * y
        mu = jnp.mean(r, axis=-1, keepdims=True)
        rc = r - mu
        var = jnp.mean(rc * rc, axis=-1, keepdims=True)
        o_ref[rows, :] = rc * lax.rsqrt(var + NORM_EPS) * g_ref[...] + b_ref[...]


def _output_stage(y_f, y_a, x2d, gate, w_out_bf16, ln_g, ln_b, layer):
    tm = 512
    nt = SEQ // tm
    row = lambda i: (i, 0)
    return pl.pallas_call(
        _out_kernel,
        out_shape=jax.ShapeDtypeStruct((TOKENS, D_MODEL), F32),
        grid=(TOKENS // tm,),
        in_specs=[
            pl.BlockSpec((1, FOURIER_GROUPS, tm, FOURIER_GROUP_DIM),
                         lambda i: (i // nt, 0, i % nt, 0)),
            pl.BlockSpec((tm, D_ATTN), row),
            pl.BlockSpec((tm, D_MODEL), row),
            pl.BlockSpec((1, 1, D_MODEL), lambda i: (i // nt, 0, 0)),
            _layer_resident((D_MODEL, D_MODEL), layer),
            _resident((1, D_MODEL)),
            _resident((1, D_MODEL)),
        ],
        out_specs=pl.BlockSpec((tm, D_MODEL), row),
        compiler_params=pltpu.CompilerParams(
            dimension_semantics=("arbitrary",), vmem_limit_bytes=VMEM_LIMIT),
        name="output_stage",
    )(y_f, y_a, x2d, gate, w_out_bf16, ln_g, ln_b)


def _w_in_kernel(wt_ref, o_ref):
    pad = W_IN_COLS - wt_ref.shape[1]
    split = COL_ZA - pad
    aligned = COL_ZA - 128
    o_ref[0, :, 0:aligned] = wt_ref[0, 0:aligned, :].T.astype(BF16)
    mid = wt_ref[0, aligned:aligned + 128, :].T
    lane = lax.broadcasted_iota(jnp.int32, mid.shape, 1)
    o_ref[0, :, aligned:COL_ZA] = jnp.where(lane < split - aligned, mid, 0.0).astype(BF16)
    o_ref[0, :, COL_ZA:W_IN_COLS] = wt_ref[0, split:, :].T.astype(BF16)


def _prepare_w_in(w_in):
    rows = 512
    cols = w_in.shape[2]
    return pl.pallas_call(
        _w_in_kernel,
        out_shape=jax.ShapeDtypeStruct((DEPTH, D_MODEL, W_IN_COLS), BF16),
        grid=(DEPTH, D_MODEL // rows),
        in_specs=[pl.BlockSpec((1, cols, rows), lambda l, i: (l, 0, i))],
        out_specs=pl.BlockSpec((1, rows, W_IN_COLS), lambda l, i: (l, i, 0)),
        compiler_params=pltpu.CompilerParams(
            dimension_semantics=("arbitrary", "arbitrary"), vmem_limit_bytes=VMEM_LIMIT),
        name="prepare_w_in",
    )(jnp.swapaxes(w_in, 1, 2))


def _layer_weights(q_norm_l, w_q_b_l, kv_norm_l, w_kv_b_l, w_fmix_l):
    wq = w_q_b_l.reshape(Q_LORA_RANK, N_HEADS, QK_HEAD_DIM)
    wq = jnp.concatenate([
        wq[:, :, :QK_NOPE_DIM].reshape(Q_LORA_RANK, -1),
        wq[:, :, QK_NOPE_DIM:QK_NOPE_DIM + HALF_ROPE].reshape(Q_LORA_RANK, -1),
        wq[:, :, QK_NOPE_DIM + HALF_ROPE:].reshape(Q_LORA_RANK, -1)], axis=1)
    wkv = w_kv_b_l.reshape(KV_LORA_RANK, N_HEADS, QK_NOPE_DIM + V_HEAD_DIM)
    wk = wkv[:, :, :QK_NOPE_DIM].reshape(KV_LORA_RANK, -1)
    wv = wkv[:, :, QK_NOPE_DIM:].reshape(KV_LORA_RANK, -1)
    return {
        "q_norm": q_norm_l.reshape(1, Q_LORA_RANK),
        "kv_norm": kv_norm_l.reshape(1, KV_LORA_RANK),
        "wq_t": wq.T.astype(BF16),
        "wk": wk.astype(BF16),
        "wv_t": wv.T.astype(BF16),
        "ab": _fold_fourier_weights(w_fmix_l),
    }


def kernel(x, c, positions, w_ada, b_ada, w_in, q_norm, w_q_b, kv_norm, w_kv_b, w_fmix, w_out,
           ln_g, ln_b):
    assert x.shape == (BATCH, SEQ, D_MODEL) and w_ada.shape[0] == DEPTH
    mod = _ada_modulation(c, w_ada, b_ada)
    tables = _rope_tables(positions)
    dft_consts = _dft_constants()
    w_in = _prepare_w_in(w_in)
    w_out_bf16 = w_out.astype(BF16)
    x2d = x.reshape(TOKENS, D_MODEL)
    for l in range(DEPTH):
        shift, scale, gate = (
            mod[l, :BATCH, i * D_MODEL:(i + 1) * D_MODEL].reshape(BATCH, 1, D_MODEL)
            for i in range(3))
        w = _layer_weights(q_norm[l], w_q_b[l], kv_norm[l], w_kv_b[l], w_fmix[l])
        pq, gate_f, gate_a, q_t, k, v_t = _token_stage(x2d, scale, shift, tables, w, w_in, l)
        y_f = _seq_dft(pq, gate_f, dft_consts)
        y_a = _attention(q_t, k, v_t, gate_a)
        x2d = _output_stage(y_f, y_a, x2d, gate, w_out_bf16,
                            ln_g[l].reshape(1, D_MODEL), ln_b[l].reshape(1, D_MODEL), l)
    return x2d.reshape(BATCH, SEQ, D_MODEL)
```

```python
import functools
import math

import numpy as np
import jax
import jax.numpy as jnp
from jax import lax
from jax.experimental import pallas as pl
from jax.experimental.pallas import tpu as pltpu

D_MODEL = 2048
BATCH = 4
SEQ = 4096
DEPTH = 2
D_FOURIER = D_MODEL // 2
FOURIER_GROUPS = 8
FOURIER_GROUP_DIM = D_FOURIER // FOURIER_GROUPS
V_HEAD_DIM = 128
D_ATTN = D_MODEL // 2
N_HEADS = D_ATTN // V_HEAD_DIM
QK_NOPE_DIM = 128
QK_ROPE_DIM = 64
QK_HEAD_DIM = QK_NOPE_DIM + QK_ROPE_DIM
Q_LORA_RANK = D_MODEL // 4
KV_LORA_RANK = D_MODEL // 8
ROPE_THETA = 10000.0
NORM_EPS = 1e-6
DEEPNORM_ALPHA = (2 * DEPTH) ** 0.25

TOKENS = BATCH * SEQ
HALF_ROPE = QK_ROPE_DIM // 2
QK_PAD_DIM = 256
W_IN_COLS = 3968
COL_U, COL_ZF, COL_CQ, COL_CKV, COL_ZA = 0, 1024, 2048, 2560, 2944

F32 = jnp.float32
BF16 = jnp.bfloat16
VMEM_LIMIT = 52 * 1024 * 1024

Q_SCALE = math.log2(math.e) / math.sqrt(QK_HEAD_DIM)


def _silu(v):
    return v * (1.0 / (1.0 + jnp.exp(-v)))


def _nt_dot(a, b):
    return lax.dot_general(a, b, (((1,), (1,)), ((), ())), preferred_element_type=F32)


def _ada_kernel(c_ref, w_ref, b_ref, o_ref):
    c_act = _silu(c_ref[...]).astype(BF16)
    w = w_ref[0].astype(BF16)
    o_ref[0] = jnp.dot(c_act, w, preferred_element_type=F32) + b_ref[0]


def _ada_modulation(c, w_ada, b_ada):
    tn = 768
    c_pad = jnp.pad(c, ((0, 8 - BATCH), (0, 0)))
    b3 = b_ada.reshape(DEPTH, 1, 3 * D_MODEL)
    return pl.pallas_call(
        _ada_kernel,
        out_shape=jax.ShapeDtypeStruct((DEPTH, 8, 3 * D_MODEL), F32),
        grid=(DEPTH, 3 * D_MODEL // tn),
        in_specs=[
            pl.BlockSpec((8, D_MODEL), lambda l, j: (0, 0)),
            pl.BlockSpec((1, D_MODEL, tn), lambda l, j: (l, 0, j)),
            pl.BlockSpec((1, 1, tn), lambda l, j: (l, 0, j)),
        ],
        out_specs=pl.BlockSpec((1, 8, tn), lambda l, j: (l, 0, j)),
        compiler_params=pltpu.CompilerParams(
            dimension_semantics=("arbitrary", "arbitrary"), vmem_limit_bytes=VMEM_LIMIT),
        name="ada_modulation",
    )(c_pad, w_ada, b3)


def _rope_kernel(pos_ref, invf_ref, cos_t_ref, sin_t_ref):
    ang_t = invf_ref[...] * pos_ref[0].astype(F32)
    cos_t_ref[0] = jnp.cos(ang_t)
    sin_t_ref[0] = jnp.sin(ang_t)


def _rope_tables(positions):
    ts = 2048
    inv_freq = ROPE_THETA ** (-jnp.arange(0, QK_ROPE_DIM, 2, dtype=F32) / QK_ROPE_DIM)
    table = jax.ShapeDtypeStruct((BATCH, HALF_ROPE, SEQ), F32)
    return pl.pallas_call(
        _rope_kernel,
        out_shape=(table, table),
        grid=(BATCH, SEQ // ts),
        in_specs=[
            pl.BlockSpec((1, 1, ts), lambda b, s: (b, 0, s)),
            pl.BlockSpec((HALF_ROPE, 1), lambda b, s: (0, 0)),
        ],
        out_specs=(
            pl.BlockSpec((1, HALF_ROPE, ts), lambda b, s: (b, 0, s)),
            pl.BlockSpec((1, HALF_ROPE, ts), lambda b, s: (b, 0, s)),
        ),
        compiler_params=pltpu.CompilerParams(
            dimension_semantics=("arbitrary", "arbitrary"), vmem_limit_bytes=VMEM_LIMIT),
        name="rope_tables",
    )(positions.reshape(BATCH, 1, SEQ), inv_freq.reshape(HALF_ROPE, 1))


def _channel_dft_matrix():
    idx = np.arange(FOURIER_GROUP_DIM)
    ang = 2.0 * np.pi * ((idx[:, None] * idx[None, :]) % FOURIER_GROUP_DIM) / FOURIER_GROUP_DIM
    scale = 1.0 / math.sqrt(FOURIER_GROUP_DIM)
    return np.concatenate([np.cos(ang), np.sin(ang)], axis=1).astype(np.float32) * np.float32(scale)


def _fold_kernel(cs_ref, w_ref, o_ref):
    w = w_ref[0]
    dims = (((1,), (0,)), ((), ()))
    a = lax.dot_general(cs_ref[:, :FOURIER_GROUP_DIM], w, dims,
                        precision=lax.Precision.HIGHEST, preferred_element_type=F32)
    b = lax.dot_general(cs_ref[:, FOURIER_GROUP_DIM:], w, dims,
                        precision=lax.Precision.HIGHEST, preferred_element_type=F32)
    o_ref[0] = jnp.concatenate([a, b], axis=1).astype(BF16)


def _fold_fourier_weights(w_fmix_l):
    g, c = FOURIER_GROUPS, FOURIER_GROUP_DIM
    return pl.pallas_call(
        _fold_kernel,
        out_shape=jax.ShapeDtypeStruct((g, c, 2 * c), BF16),
        grid=(g,),
        in_specs=[
            pl.BlockSpec((c, 2 * c), lambda i: (0, 0)),
            pl.BlockSpec((1, c, c), lambda i: (i, 0, 0)),
        ],
        out_specs=pl.BlockSpec((1, c, 2 * c), lambda i: (i, 0, 0)),
        compiler_params=pltpu.CompilerParams(dimension_semantics=("arbitrary",)),
        name="fold_fourier_weights",
    )(jnp.asarray(_channel_dft_matrix()), w_fmix_l)


def _token_kernel(x_ref, scale_ref, shift_ref, cos_t_ref, sin_t_ref,
                  w_in_ref, qn_ref, kvn_ref, wq_t_ref, wk_ref, wv_t_ref, ab_ref,
                  pq_ref, gf_ref, ga_ref, q_t_ref, k_ref, v_t_ref):
    x = x_ref[...]
    tm = x.shape[0]
    mu = jnp.mean(x, axis=-1, keepdims=True)
    xc = x - mu
    var = jnp.mean(xc * xc, axis=-1, keepdims=True)
    h = xc * lax.rsqrt(var + NORM_EPS) * (1.0 + scale_ref[0]) + shift_ref[0]
    hb = h.astype(BF16)

    def proj(lo, hi):
        return jnp.dot(hb, w_in_ref[0, :, lo:hi], preferred_element_type=F32)

    ub = proj(COL_U, COL_ZF).astype(BF16)
    c = FOURIER_GROUP_DIM
    gate_f = _silu(proj(COL_ZF, COL_CQ)).astype(BF16)
    for g in range(FOURIER_GROUPS):
        pq = jnp.dot(ub[:, g * c:(g + 1) * c], ab_ref[g], preferred_element_type=F32)
        pq_ref[0, 0, g] = pq[:, :c].astype(BF16)
        pq_ref[0, 1, g] = pq[:, c:].astype(BF16)
        gf_ref[0, g] = gate_f[:, g * c:(g + 1) * c]

    ga_ref[...] = _silu(proj(COL_ZA, W_IN_COLS)).astype(BF16)

    cq = proj(COL_CQ, COL_CKV)
    cqn = (cq * lax.rsqrt(jnp.mean(cq * cq, axis=-1, keepdims=True) + NORM_EPS)
           * qn_ref[...]).astype(BF16)
    q_t = _nt_dot(wq_t_ref[...], cqn) * Q_SCALE
    cos_t = cos_t_ref[0]
    sin_t = sin_t_ref[0]
    n_nope = N_HEADS * QK_NOPE_DIM
    n_half = N_HEADS * HALF_ROPE
    zeros_q = jnp.zeros((QK_PAD_DIM - QK_HEAD_DIM, tm), BF16)
    for hd in range(N_HEADS):
        q_t_ref[0, hd, 0:QK_NOPE_DIM, :] = (
            q_t[hd * QK_NOPE_DIM:(hd + 1) * QK_NOPE_DIM].astype(BF16))
        x1 = q_t[n_nope + hd * HALF_ROPE:n_nope + (hd + 1) * HALF_ROPE]
        x2 = q_t[n_nope + n_half + hd * HALF_ROPE:n_nope + n_half + (hd + 1) * HALF_ROPE]
        q_t_ref[0, hd, QK_NOPE_DIM:QK_NOPE_DIM + HALF_ROPE, :] = (
            x1 * cos_t - x2 * sin_t).astype(BF16)
        q_t_ref[0, hd, QK_NOPE_DIM + HALF_ROPE:QK_HEAD_DIM, :] = (
            x2 * cos_t + x1 * sin_t).astype(BF16)
        q_t_ref[0, hd, QK_HEAD_DIM:QK_PAD_DIM, :] = zeros_q

    ckv_kr = proj(COL_CKV, COL_ZA)
    ckv = ckv_kr[:, :KV_LORA_RANK]
    ckvn = (ckv * lax.rsqrt(jnp.mean(ckv * ckv, axis=-1, keepdims=True) + NORM_EPS)
            * kvn_ref[...]).astype(BF16)
    k_nope = jnp.dot(ckvn, wk_ref[...], preferred_element_type=F32)
    v_t = _nt_dot(wv_t_ref[...], ckvn)
    kr_t = ckv_kr[:, KV_LORA_RANK:].T
    k1, k2 = kr_t[:HALF_ROPE], kr_t[HALF_ROPE:QK_ROPE_DIM]
    pad_row = lax.broadcasted_iota(jnp.int32, (QK_PAD_DIM - QK_HEAD_DIM, tm), 0)
    kr_pad = jnp.concatenate(
        [k1 * cos_t - k2 * sin_t, k2 * cos_t + k1 * sin_t, jnp.where(pad_row == 0, 1.0, 0.0)],
        axis=0).T.astype(BF16)
    for hd in range(N_HEADS):
        k_ref[0, hd, :, 0:QK_NOPE_DIM] = (
            k_nope[:, hd * QK_NOPE_DIM:(hd + 1) * QK_NOPE_DIM].astype(BF16))
        k_ref[0, hd, :, QK_NOPE_DIM:QK_PAD_DIM] = kr_pad
        v_t_ref[0, hd] = v_t[hd * V_HEAD_DIM:(hd + 1) * V_HEAD_DIM].astype(BF16)


def _resident(shape):
    zeros = (0,) * len(shape)
    return pl.BlockSpec(shape, lambda *_: zeros, pipeline_mode=pl.Buffered(1))


def _layer_resident(shape, layer):
    index = (layer,) + (0,) * len(shape)
    return pl.BlockSpec((1,) + tuple(shape), lambda *_: index, pipeline_mode=pl.Buffered(1))


def _token_stage(x2d, scale, shift, tables, w, w_in_all, layer):
    tm = 512
    nt = SEQ // tm
    cos_t, sin_t = tables
    row = lambda i: (i, 0)
    per_batch = lambda i: (i // nt, 0, 0)
    bh = lambda i: (i // nt, 0, 0, i % nt)
    return pl.pallas_call(
        _token_kernel,
        out_shape=(
            jax.ShapeDtypeStruct((BATCH, 2, FOURIER_GROUPS, SEQ, FOURIER_GROUP_DIM), BF16),
            jax.ShapeDtypeStruct((BATCH, FOURIER_GROUPS, SEQ, FOURIER_GROUP_DIM), BF16),
            jax.ShapeDtypeStruct((TOKENS, D_ATTN), BF16),
            jax.ShapeDtypeStruct((BATCH, N_HEADS, QK_PAD_DIM, SEQ), BF16),
            jax.ShapeDtypeStruct((BATCH, N_HEADS, SEQ, QK_PAD_DIM), BF16),
            jax.ShapeDtypeStruct((BATCH, N_HEADS, V_HEAD_DIM, SEQ), BF16),
        ),
        grid=(TOKENS // tm,),
        in_specs=[
            pl.BlockSpec((tm, D_MODEL), row),
            pl.BlockSpec((1, 1, D_MODEL), per_batch),
            pl.BlockSpec((1, 1, D_MODEL), per_batch),
            pl.BlockSpec((1, HALF_ROPE, tm), lambda i: (i // nt, 0, i % nt)),
            pl.BlockSpec((1, HALF_ROPE, tm), lambda i: (i // nt, 0, i % nt)),
            _layer_resident((D_MODEL, W_IN_COLS), layer),
            _resident((1, Q_LORA_RANK)),
            _resident((1, KV_LORA_RANK)),
            _resident((N_HEADS * QK_HEAD_DIM, Q_LORA_RANK)),
            _resident((KV_LORA_RANK, N_HEADS * QK_NOPE_DIM)),
            _resident((N_HEADS * V_HEAD_DIM, KV_LORA_RANK)),
            _resident((FOURIER_GROUPS, FOURIER_GROUP_DIM, 2 * FOURIER_GROUP_DIM)),
        ],
        out_specs=(
            pl.BlockSpec((1, 2, FOURIER_GROUPS, tm, FOURIER_GROUP_DIM),
                         lambda i: (i // nt, 0, 0, i % nt, 0)),
            pl.BlockSpec((1, FOURIER_GROUPS, tm, FOURIER_GROUP_DIM),
                         lambda i: (i // nt, 0, i % nt, 0)),
            pl.BlockSpec((tm, D_ATTN), row),
            pl.BlockSpec((1, N_HEADS, QK_PAD_DIM, tm), bh),
            pl.BlockSpec((1, N_HEADS, tm, QK_PAD_DIM), lambda i: (i // nt, 0, i % nt, 0)),
            pl.BlockSpec((1, N_HEADS, V_HEAD_DIM, tm), bh),
        ),
        compiler_params=pltpu.CompilerParams(
            dimension_semantics=("arbitrary",), vmem_limit_bytes=VMEM_LIMIT),
        name="token_stage",
    )(x2d, scale, shift, cos_t, sin_t,
      w_in_all, w["q_norm"], w["kv_norm"], w["wq_t"], w["wk"], w["wv_t"], w["ab"])


DFT_LEVELS = 3
DFT_BLOCKS = 2 ** DFT_LEVELS
DFT_SUB = SEQ // DFT_BLOCKS
DFT_GROUPS_PER_STEP = 2
DFT_PASSES = ((0, 3),)


def _dft_constants():
    tw = []
    for level in range(DFT_LEVELS):
        length = SEQ >> level
        ang = 2.0 * np.pi * np.arange(length // 2) / length
        for f in (np.cos, np.sin):
            col = jnp.asarray(f(ang).astype(np.float32)).reshape(-1, 1)
            tw.append(jnp.broadcast_to(col, (length // 2, FOURIER_GROUP_DIM)))
    idx = np.arange(DFT_SUB)
    ang = 2.0 * np.pi * ((idx[:, None] * idx[None, :]) % DFT_SUB) / DFT_SUB
    mat = np.concatenate([np.cos(ang), -np.sin(ang)], axis=1) / math.sqrt(SEQ)
    return tw, jnp.asarray(mat.astype(np.float32)).astype(BF16)


def _bit_reverse(j, bits):
    return int(format(j, "0{}b".format(bits))[::-1], 2)


def _dft_kernel(*refs):
    tw_refs = refs[:2 * DFT_LEVELS]
    m_ref, pq_ref, g_ref, o_ref, p_scr, w_scr = refs[2 * DFT_LEVELS:2 * DFT_LEVELS + 6]
    y_scr = refs[2 * DFT_LEVELS + 6:]
    gps, c = DFT_GROUPS_PER_STEP, FOURIER_GROUP_DIM

    for first_level, n_levels in DFT_PASSES:
        from_input = first_level == 0
        rc = 16 if from_input else 8
        n_blocks = 2 ** n_levels
        blk = SEQ >> (first_level + n_levels)

        def body(i, carry, first_level=first_level, n_levels=n_levels, from_input=from_input,
                 rc=rc, n_blocks=n_blocks, blk=blk):
            r = pl.multiple_of(i * rc, rc)
            tw = {}
            for t in range(n_levels):
                for j in range(n_blocks >> (t + 1)):
                    rows = pl.ds(j * blk + r, rc)
                    tw[t, j] = (tw_refs[2 * (first_level + t)][rows, :],
                                tw_refs[2 * (first_level + t) + 1][rows, :])
            for sub in range(2 ** first_level):
                base = sub * n_blocks * blk
                for g in range(gps):
                    lanes = slice(g * c, (g + 1) * c)
                    data = []
                    for j in range(n_blocks):
                        rows = pl.ds(base + j * blk + r, rc)
                        if from_input:
                            data.append((pq_ref[0, 0, g, rows, :].astype(F32),
                                         pq_ref[0, 1, g, rows, :].astype(F32)))
                        else:
                            data.append((p_scr[rows, lanes], w_scr[rows, lanes]))
                    for t in range(n_levels):
                        half = n_blocks >> (t + 1)
                        for g0 in range(0, n_blocks, 2 * half):
                            for j in range(half):
                                (pa, wa), (pb, wb) = data[g0 + j], data[g0 + j + half]
                                cos, sin = tw[t, j]
                                dp, dw = pa - pb, wa - wb
                                data[g0 + j] = (pa + pb, wa + wb)
                                data[g0 + j + half] = (dp * cos - dw * sin, dw * cos + dp * sin)
                    for j in range(n_blocks):
                        rows = pl.ds(base + j * blk + r, rc)
                        p_scr[rows, lanes] = data[j][0]
                        w_scr[rows, lanes] = data[j][1]
            return carry

        lax.fori_loop(0, blk // rc, body, 0, unroll=2)

    for j in range(DFT_BLOCKS):
        rows = slice(j * DFT_SUB, (j + 1) * DFT_SUB)
        z = jnp.concatenate([p_scr[rows, :], w_scr[rows, :]], axis=0).astype(BF16)
        y = jnp.dot(m_ref[...], z, preferred_element_type=F32)
        out_rows = pl.ds(_bit_reverse(j, DFT_LEVELS), DFT_SUB, stride=DFT_BLOCKS)
        for g in range(gps):
            y_scr[g][out_rows, :] = y[:, g * c:(g + 1) * c]
    for g in range(gps):
        o_ref[0, g] = (y_scr[g][...] * g_ref[0, g].astype(F32)).astype(BF16)


def _seq_dft(pq, gate_f, consts):
    tw, mat = consts
    gps, c = DFT_GROUPS_PER_STEP, FOURIER_GROUP_DIM
    n_gsteps = FOURIER_GROUPS // gps
    return pl.pallas_call(
        _dft_kernel,
        out_shape=jax.ShapeDtypeStruct((BATCH, FOURIER_GROUPS, SEQ, c), BF16),
        grid=(BATCH, n_gsteps),
        in_specs=(
            [_resident(t.shape) for t in tw]
            + [_resident(mat.shape),
               pl.BlockSpec((1, 2, gps, SEQ, c), lambda b, gi: (b, 0, gi, 0, 0)),
               pl.BlockSpec((1, gps, SEQ, c), lambda b, gi: (b, gi, 0, 0))]),
        out_specs=pl.BlockSpec((1, gps, SEQ, c), lambda b, gi: (b, gi, 0, 0)),
        scratch_shapes=(
            [pltpu.VMEM((SEQ, gps * c), F32)] * 2 + [pltpu.VMEM((SEQ, c), F32)] * gps),
        compiler_params=pltpu.CompilerParams(
            dimension_semantics=("arbitrary", "arbitrary"), vmem_limit_bytes=VMEM_LIMIT),
        name="seq_dft",
    )(*tw, mat, pq, gate_f)


ATTN_PROBE_KEYS = 128
ATTN_MAX_DENOM = 2.0 ** 16


def _attn_kernel(q_t_ref, k_ref, v_t_ref, g_ref, o_ref, q_aug, s_scr, p_scr, *, tk):
    tq = q_t_ref.shape[3]
    n_chunks = SEQ // tk

    def finish(acc, l):
        o = (acc * (1.0 / l)).T
        o_ref[...] = (o * g_ref[...].astype(F32)).astype(BF16)

    s_probe = jnp.dot(k_ref[0, 0, 0:ATTN_PROBE_KEYS, :], q_t_ref[0, 0], preferred_element_type=F32)
    stab = jnp.max(s_probe, axis=0, keepdims=True)
    q_aug[...] = q_t_ref[0, 0]
    pad_rows = QK_PAD_DIM - QK_HEAD_DIM
    first = lax.broadcasted_iota(jnp.int32, (pad_rows, tq), 0) == 0
    q_aug[QK_HEAD_DIM:QK_PAD_DIM, :] = jnp.where(first, -stab, 0.0).astype(BF16)
    def scores(c):
        s_scr[c % 2] = jnp.dot(k_ref[0, 0, c * tk:(c + 1) * tk, :], q_aug[...],
                               preferred_element_type=F32)

    def probs(c):
        p = jnp.exp2(s_scr[c % 2])
        p_scr[c % 2] = p.astype(BF16)
        return jnp.sum(p, axis=0, keepdims=True)

    def values(c):
        return jnp.dot(v_t_ref[0, 0, :, c * tk:(c + 1) * tk], p_scr[c % 2],
                       preferred_element_type=F32)

    scores(0)
    scores(1)
    l = probs(0)
    acc = jnp.zeros((V_HEAD_DIM, tq), F32)
    for c in range(1, n_chunks):
        if c + 1 < n_chunks:
            scores(c + 1)
        l = l + probs(c)
        acc = acc + values(c - 1)
    acc = acc + values(n_chunks - 1)
    finish(acc, l)

    @pl.when(jnp.logical_not(jnp.max(l) <= ATTN_MAX_DENOM))
    def _():
        m = jnp.full((1, tq), -jnp.inf, F32)
        l2 = jnp.zeros((1, tq), F32)
        acc2 = jnp.zeros((V_HEAD_DIM, tq), F32)
        for c in range(n_chunks):
            s = jnp.dot(k_ref[0, 0, c * tk:(c + 1) * tk, :], q_t_ref[0, 0],
                        preferred_element_type=F32)
            m_new = jnp.maximum(m, jnp.max(s, axis=0, keepdims=True))
            alpha = jnp.exp2(m - m_new)
            p = jnp.exp2(s - m_new)
            l2 = alpha * l2 + jnp.sum(p, axis=0, keepdims=True)
            acc2 = alpha * acc2 + jnp.dot(v_t_ref[0, 0, :, c * tk:(c + 1) * tk], p.astype(BF16),
                                          preferred_element_type=F32)
            m = m_new
        finish(acc2, l2)


def _attention(q_t, k, v_t, gate_a):
    tq, tk = 2048, 512
    nq = SEQ // tq
    return pl.pallas_call(
        functools.partial(_attn_kernel, tk=tk),
        scratch_shapes=[
            pltpu.VMEM((QK_PAD_DIM, tq), BF16),
            pltpu.VMEM((2, tk, tq), F32),
            pltpu.VMEM((2, tk, tq), BF16),
        ],
        out_shape=jax.ShapeDtypeStruct((TOKENS, D_ATTN), BF16),
        grid=(BATCH, N_HEADS, nq),
        in_specs=[
            pl.BlockSpec((1, 1, QK_PAD_DIM, tq), lambda b, h, i: (b, h, 0, i)),
            pl.BlockSpec((1, 1, SEQ, QK_PAD_DIM), lambda b, h, i: (b, h, 0, 0)),
            pl.BlockSpec((1, 1, V_HEAD_DIM, SEQ), lambda b, h, i: (b, h, 0, 0)),
            pl.BlockSpec((tq, V_HEAD_DIM), lambda b, h, i: (b * nq + i, h)),
        ],
        out_specs=pl.BlockSpec((tq, V_HEAD_DIM), lambda b, h, i: (b * nq + i, h)),
        compiler_params=pltpu.CompilerParams(
            dimension_semantics=("arbitrary", "arbitrary", "arbitrary"),
            vmem_limit_bytes=VMEM_LIMIT),
        name="attention",
    )(q_t, k, v_t, gate_a)


OUT_SUB_ROWS = 256


def _out_kernel(yf_ref, ya_ref, x_ref, gate_ref, w_ref, g_ref, b_ref, o_ref):
    for rows in (slice(i * OUT_SUB_ROWS, (i + 1) * OUT_SUB_ROWS)
                 for i in range(x_ref.shape[0] // OUT_SUB_ROWS)):
        y_f = jnp.concatenate([yf_ref[0, g, rows, :] for g in range(FOURIER_GROUPS)], axis=1)
        y = (jnp.dot(y_f, w_ref[0, :D_FOURIER, :], preferred_element_type=F32)
             + jnp.dot(ya_ref[rows, :], w_ref[0, D_FOURIER:, :], preferred_element_type=F32))
        r = DEEPNORM_ALPHA * x_ref[rows, :] + gate_ref[0] * y
        mu = jnp.mean(r, axis=-1, keepdims=True)
        rc = r - mu
        var = jnp.mean(rc * rc, axis=-1, keepdims=True)
        o_ref[rows, :] = rc * lax.rsqrt(var + NORM_EPS) * g_ref[...] + b_ref[...]


def _output_stage(y_f, y_a, x2d, gate, w_out_bf16, ln_g, ln_b, layer):
    tm = 512
    nt = SEQ // tm
    row = lambda i: (i, 0)
    return pl.pallas_call(
        _out_kernel,
        out_shape=jax.ShapeDtypeStruct((TOKENS, D_MODEL), F32),
        grid=(TOKENS // tm,),
        in_specs=[
            pl.BlockSpec((1, FOURIER_GROUPS, tm, FOURIER_GROUP_DIM),
                         lambda i: (i // nt, 0, i % nt, 0)),
            pl.BlockSpec((tm, D_ATTN), row),
            pl.BlockSpec((tm, D_MODEL), row),
            pl.BlockSpec((1, 1, D_MODEL), lambda i: (i // nt, 0, 0)),
            _layer_resident((D_MODEL, D_MODEL), layer),
            _resident((1, D_MODEL)),
            _resident((1, D_MODEL)),
        ],
        out_specs=pl.BlockSpec((tm, D_MODEL), row),
        compiler_params=pltpu.CompilerParams(
            dimension_semantics=("arbitrary",), vmem_limit_bytes=VMEM_LIMIT),
        name="output_stage",
    )(y_f, y_a, x2d, gate, w_out_bf16, ln_g, ln_b)


def _w_in_kernel(wt_ref, o_ref):
    pad = W_IN_COLS - wt_ref.shape[1]
    split = COL_ZA - pad
    aligned = COL_ZA - 128
    o_ref[0, :, 0:aligned] = wt_ref[0, 0:aligned, :].T.astype(BF16)
    mid = wt_ref[0, aligned:aligned + 128, :].T
    lane = lax.broadcasted_iota(jnp.int32, mid.shape, 1)
    o_ref[0, :, aligned:COL_ZA] = jnp.where(lane < split - aligned, mid, 0.0).astype(BF16)
    o_ref[0, :, COL_ZA:W_IN_COLS] = wt_ref[0, split:, :].T.astype(BF16)


def _prepare_w_in(w_in):
    rows = 512
    cols = w_in.shape[2]
    return pl.pallas_call(
        _w_in_kernel,
        out_shape=jax.ShapeDtypeStruct((DEPTH, D_MODEL, W_IN_COLS), BF16),
        grid=(DEPTH, D_MODEL // rows),
        in_specs=[pl.BlockSpec((1, cols, rows), lambda l, i: (l, 0, i))],
        out_specs=pl.BlockSpec((1, rows, W_IN_COLS), lambda l, i: (l, i, 0)),
        compiler_params=pltpu.CompilerParams(
            dimension_semantics=("arbitrary", "arbitrary"), vmem_limit_bytes=VMEM_LIMIT),
        name="prepare_w_in",
    )(jnp.swapaxes(w_in, 1, 2))


def _layer_weights(q_norm_l, w_q_b_l, kv_norm_l, w_kv_b_l, w_fmix_l):
    wq = w_q_b_l.reshape(Q_LORA_RANK, N_HEADS, QK_HEAD_DIM)
    wq = jnp.concatenate([
        wq[:, :, :QK_NOPE_DIM].reshape(Q_LORA_RANK, -1),
        wq[:, :, QK_NOPE_DIM:QK_NOPE_DIM + HALF_ROPE].reshape(Q_LORA_RANK, -1),
        wq[:, :, QK_NOPE_DIM + HALF_ROPE:].reshape(Q_LORA_RANK, -1)], axis=1)
    wkv = w_kv_b_l.reshape(KV_LORA_RANK, N_HEADS, QK_NOPE_DIM + V_HEAD_DIM)
    wk = wkv[:, :, :QK_NOPE_DIM].reshape(KV_LORA_RANK, -1)
    wv = wkv[:, :, QK_NOPE_DIM:].reshape(KV_LORA_RANK, -1)
    return {
        "q_norm": q_norm_l.reshape(1, Q_LORA_RANK),
        "kv_norm": kv_norm_l.reshape(1, KV_LORA_RANK),
        "wq_t": wq.T.astype(BF16),
        "wk": wk.astype(BF16),
        "wv_t": wv.T.astype(BF16),
        "ab": _fold_fourier_weights(w_fmix_l),
    }


def kernel(x, c, positions, w_ada, b_ada, w_in, q_norm, w_q_b, kv_norm, w_kv_b, w_fmix, w_out,
           ln_g, ln_b):
    assert x.shape == (BATCH, SEQ, D_MODEL) and w_ada.shape[0] == DEPTH
    mod = _ada_modulation(c, w_ada, b_ada)
    tables = _rope_tables(positions)
    dft_consts = _dft_constants()
    w_in = _prepare_w_in(w_in)
    w_out_bf16 = w_out.astype(BF16)
    x2d = x.reshape(TOKENS, D_MODEL)
    for l in range(DEPTH):
        shift, scale, gate = (
            mod[l, :BATCH, i * D_MODEL:(i + 1) * D_MODEL].reshape(BATCH, 1, D_MODEL)
            for i in range(3))
        w = _layer_weights(q_norm[l], w_q_b[l], kv_norm[l], w_kv_b[l], w_fmix[l])
        pq, gate_f, gate_a, q_t, k, v_t = _token_stage(x2d, scale, shift, tables, w, w_in, l)
        y_f = _seq_dft(pq, gate_f, dft_consts)
        y_a = _attention(q_t, k, v_t, gate_a)
        x2d = _output_stage(y_f, y_a, x2d, gate, w_out_bf16,
                            ln_g[l].reshape(1, D_MODEL), ln_b[l].reshape(1, D_MODEL), l)
    return x2d.reshape(BATCH, SEQ, D_MODEL)
```

```python
import functools
import math

import numpy as np
import jax
import jax.numpy as jnp
from jax import lax
from jax.experimental import pallas as pl
from jax.experimental.pallas import tpu as pltpu

D_MODEL = 2048
BATCH = 4
SEQ = 4096
DEPTH = 2
D_FOURIER = D_MODEL // 2
FOURIER_GROUPS = 8
FOURIER_GROUP_DIM = D_FOURIER // FOURIER_GROUPS
V_HEAD_DIM = 128
D_ATTN = D_MODEL // 2
N_HEADS = D_ATTN // V_HEAD_DIM
QK_NOPE_DIM = 128
QK_ROPE_DIM = 64
QK_HEAD_DIM = QK_NOPE_DIM + QK_ROPE_DIM
Q_LORA_RANK = D_MODEL // 4
KV_LORA_RANK = D_MODEL // 8
ROPE_THETA = 10000.0
NORM_EPS = 1e-6
DEEPNORM_ALPHA = (2 * DEPTH) ** 0.25

TOKENS = BATCH * SEQ
HALF_ROPE = QK_ROPE_DIM // 2
QK_PAD_DIM = 256
W_IN_COLS = 3968
COL_U, COL_ZF, COL_CQ, COL_CKV, COL_ZA = 0, 1024, 2048, 2560, 2944

F32 = jnp.float32
BF16 = jnp.bfloat16
VMEM_LIMIT = 52 * 1024 * 1024

Q_SCALE = math.log2(math.e) / math.sqrt(QK_HEAD_DIM)


def _silu(v):
    return v * (1.0 / (1.0 + jnp.exp(-v)))


def _nt_dot(a, b):
    return lax.dot_general(a, b, (((1,), (1,)), ((), ())), preferred_element_type=F32)


def _ada_kernel(c_ref, w_ref, b_ref, o_ref):
    c_act = _silu(c_ref[...]).astype(BF16)
    w = w_ref[0].astype(BF16)
    o_ref[0] = jnp.dot(c_act, w, preferred_element_type=F32) + b_ref[0]


def _ada_modulation(c, w_ada, b_ada):
    tn = 768
    c_pad = jnp.pad(c, ((0, 8 - BATCH), (0, 0)))
    b3 = b_ada.reshape(DEPTH, 1, 3 * D_MODEL)
    return pl.pallas_call(
        _ada_kernel,
        out_shape=jax.ShapeDtypeStruct((DEPTH, 8, 3 * D_MODEL), F32),
        grid=(DEPTH, 3 * D_MODEL // tn),
        in_specs=[
            pl.BlockSpec((8, D_MODEL), lambda l, j: (0, 0)),
            pl.BlockSpec((1, D_MODEL, tn), lambda l, j: (l, 0, j)),
            pl.BlockSpec((1, 1, tn), lambda l, j: (l, 0, j)),
        ],
        out_specs=pl.BlockSpec((1, 8, tn), lambda l, j: (l, 0, j)),
        compiler_params=pltpu.CompilerParams(
            dimension_semantics=("arbitrary", "arbitrary"), vmem_limit_bytes=VMEM_LIMIT),
        name="ada_modulation",
    )(c_pad, w_ada, b3)


def _rope_kernel(pos_ref, invf_ref, cos_t_ref, sin_t_ref):
    ang_t = invf_ref[...] * pos_ref[0].astype(F32)
    cos_t_ref[0] = jnp.cos(ang_t)
    sin_t_ref[0] = jnp.sin(ang_t)


def _rope_tables(positions):
    ts = 2048
    inv_freq = ROPE_THETA ** (-jnp.arange(0, QK_ROPE_DIM, 2, dtype=F32) / QK_ROPE_DIM)
    table = jax.ShapeDtypeStruct((BATCH, HALF_ROPE, SEQ), F32)
    return pl.pallas_call(
        _rope_kernel,
        out_shape=(table, table),
        grid=(BATCH, SEQ // ts),
        in_specs=[
            pl.BlockSpec((1, 1, ts), lambda b, s: (b, 0, s)),
            pl.BlockSpec((HALF_ROPE, 1), lambda b, s: (0, 0)),
        ],
        out_specs=(
            pl.BlockSpec((1, HALF_ROPE, ts), lambda b, s: (b, 0, s)),
            pl.BlockSpec((1, HALF_ROPE, ts), lambda b, s: (b, 0, s)),
        ),
        compiler_params=pltpu.CompilerParams(
            dimension_semantics=("arbitrary", "arbitrary"), vmem_limit_bytes=VMEM_LIMIT),
        name="rope_tables",
    )(positions.reshape(BATCH, 1, SEQ), inv_freq.reshape(HALF_ROPE, 1))


def _channel_dft_matrix():
    idx = np.arange(FOURIER_GROUP_DIM)
    ang = 2.0 * np.pi * ((idx[:, None] * idx[None, :]) % FOURIER_GROUP_DIM) / FOURIER_GROUP_DIM
    scale = 1.0 / math.sqrt(FOURIER_GROUP_DIM)
    return np.concatenate([np.cos(ang), np.sin(ang)], axis=1).astype(np.float32) * np.float32(scale)


def _fold_kernel(cs_ref, w_ref, o_ref):
    w = w_ref[0]
    dims = (((1,), (0,)), ((), ()))
    a = lax.dot_general(cs_ref[:, :FOURIER_GROUP_DIM], w, dims,
                        precision=lax.Precision.HIGHEST, preferred_element_type=F32)
    b = lax.dot_general(cs_ref[:, FOURIER_GROUP_DIM:], w, dims,
                        precision=lax.Precision.HIGHEST, preferred_element_type=F32)
    o_ref[0] = jnp.concatenate([a, b], axis=1).astype(BF16)


def _fold_fourier_weights(w_fmix_l):
    g, c = FOURIER_GROUPS, FOURIER_GROUP_DIM
    return pl.pallas_call(
        _fold_kernel,
        out_shape=jax.ShapeDtypeStruct((g, c, 2 * c), BF16),
        grid=(g,),
        in_specs=[
            pl.BlockSpec((c, 2 * c), lambda i: (0, 0)),
            pl.BlockSpec((1, c, c), lambda i: (i, 0, 0)),
        ],
        out_specs=pl.BlockSpec((1, c, 2 * c), lambda i: (i, 0, 0)),
        compiler_params=pltpu.CompilerParams(dimension_semantics=("arbitrary",)),
        name="fold_fourier_weights",
    )(jnp.asarray(_channel_dft_matrix()), w_fmix_l)


def _token_kernel(x_ref, scale_ref, shift_ref, cos_t_ref, sin_t_ref,
                  w_in_ref, qn_ref, kvn_ref, wq_t_ref, wk_ref, wv_t_ref, ab_ref,
                  pq_ref, gf_ref, ga_ref, q_t_ref, k_ref, v_t_ref):
    x = x_ref[...]
    tm = x.shape[0]
    mu = jnp.mean(x, axis=-1, keepdims=True)
    xc = x - mu
    var = jnp.mean(xc * xc, axis=-1, keepdims=True)
    h = xc * lax.rsqrt(var + NORM_EPS) * (1.0 + scale_ref[0]) + shift_ref[0]
    hb = h.astype(BF16)

    def proj(lo, hi):
        return jnp.dot(hb, w_in_ref[0, :, lo:hi], preferred_element_type=F32)

    p_a = proj(COL_U, COL_CQ)
    p_b = proj(COL_CQ, W_IN_COLS)
    ub = p_a[:, :COL_ZF].astype(BF16)
    c = FOURIER_GROUP_DIM
    gate_f = _silu(p_a[:, COL_ZF:]).astype(BF16)
    for g in range(FOURIER_GROUPS):
        pq = jnp.dot(ub[:, g * c:(g + 1) * c], ab_ref[g], preferred_element_type=F32)
        pq_ref[0, 0, g] = pq[:, :c].astype(BF16)
        pq_ref[0, 1, g] = pq[:, c:].astype(BF16)
        gf_ref[0, g] = gate_f[:, g * c:(g + 1) * c]

    ga_ref[...] = _silu(p_b[:, COL_ZA - COL_CQ:]).astype(BF16)

    cq = p_b[:, :COL_CKV - COL_CQ]
    cqn = (cq * lax.rsqrt(jnp.mean(cq * cq, axis=-1, keepdims=True) + NORM_EPS)
           * qn_ref[...]).astype(BF16)
    q_t = _nt_dot(wq_t_ref[...], cqn) * Q_SCALE
    cos_t = cos_t_ref[0]
    sin_t = sin_t_ref[0]
    n_nope = N_HEADS * QK_NOPE_DIM
    n_half = N_HEADS * HALF_ROPE
    zeros_q = jnp.zeros((QK_PAD_DIM - QK_HEAD_DIM, tm), BF16)
    for hd in range(N_HEADS):
        q_t_ref[0, hd, 0:QK_NOPE_DIM, :] = (
            q_t[hd * QK_NOPE_DIM:(hd + 1) * QK_NOPE_DIM].astype(BF16))
        x1 = q_t[n_nope + hd * HALF_ROPE:n_nope + (hd + 1) * HALF_ROPE]
        x2 = q_t[n_nope + n_half + hd * HALF_ROPE:n_nope + n_half + (hd + 1) * HALF_ROPE]
        q_t_ref[0, hd, QK_NOPE_DIM:QK_NOPE_DIM + HALF_ROPE, :] = (
            x1 * cos_t - x2 * sin_t).astype(BF16)
        q_t_ref[0, hd, QK_NOPE_DIM + HALF_ROPE:QK_HEAD_DIM, :] = (
            x2 * cos_t + x1 * sin_t).astype(BF16)
        q_t_ref[0, hd, QK_HEAD_DIM:QK_PAD_DIM, :] = zeros_q

    ckv_kr = p_b[:, COL_CKV - COL_CQ:COL_ZA - COL_CQ]
    ckv = ckv_kr[:, :KV_LORA_RANK]
    ckvn = (ckv * lax.rsqrt(jnp.mean(ckv * ckv, axis=-1, keepdims=True) + NORM_EPS)
            * kvn_ref[...]).astype(BF16)
    k_nope = jnp.dot(ckvn, wk_ref[...], preferred_element_type=F32)
    v_t = _nt_dot(wv_t_ref[...], ckvn)
    kr_t = ckv_kr[:, KV_LORA_RANK:].T
    k1, k2 = kr_t[:HALF_ROPE], kr_t[HALF_ROPE:QK_ROPE_DIM]
    pad_row = lax.broadcasted_iota(jnp.int32, (QK_PAD_DIM - QK_HEAD_DIM, tm), 0)
    kr_pad = jnp.concatenate(
        [k1 * cos_t - k2 * sin_t, k2 * cos_t + k1 * sin_t, jnp.where(pad_row == 0, 1.0, 0.0)],
        axis=0).T.astype(BF16)
    for hd in range(N_HEADS):
        k_ref[0, hd, :, 0:QK_NOPE_DIM] = (
            k_nope[:, hd * QK_NOPE_DIM:(hd + 1) * QK_NOPE_DIM].astype(BF16))
        k_ref[0, hd, :, QK_NOPE_DIM:QK_PAD_DIM] = kr_pad
        v_t_ref[0, hd] = v_t[hd * V_HEAD_DIM:(hd + 1) * V_HEAD_DIM].astype(BF16)


def _resident(shape):
    zeros = (0,) * len(shape)
    return pl.BlockSpec(shape, lambda *_: zeros, pipeline_mode=pl.Buffered(1))


def _layer_resident(shape, layer):
    index = (layer,) + (0,) * len(shape)
    return pl.BlockSpec((1,) + tuple(shape), lambda *_: index, pipeline_mode=pl.Buffered(1))


def _token_stage(x2d, scale, shift, tables, w, w_in_all, layer):
    tm = 512
    nt = SEQ // tm
    cos_t, sin_t = tables
    row = lambda i: (i, 0)
    per_batch = lambda i: (i // nt, 0, 0)
    bh = lambda i: (i // nt, 0, 0, i % nt)
    return pl.pallas_call(
        _token_kernel,
        out_shape=(
            jax.ShapeDtypeStruct((BATCH, 2, FOURIER_GROUPS, SEQ, FOURIER_GROUP_DIM), BF16),
            jax.ShapeDtypeStruct((BATCH, FOURIER_GROUPS, SEQ, FOURIER_GROUP_DIM), BF16),
            jax.ShapeDtypeStruct((TOKENS, D_ATTN), BF16),
            jax.ShapeDtypeStruct((BATCH, N_HEADS, QK_PAD_DIM, SEQ), BF16),
            jax.ShapeDtypeStruct((BATCH, N_HEADS, SEQ, QK_PAD_DIM), BF16),
            jax.ShapeDtypeStruct((BATCH, N_HEADS, V_HEAD_DIM, SEQ), BF16),
        ),
        grid=(TOKENS // tm,),
        in_specs=[
            pl.BlockSpec((tm, D_MODEL), row),
            pl.BlockSpec((1, 1, D_MODEL), per_batch),
            pl.BlockSpec((1, 1, D_MODEL), per_batch),
            pl.BlockSpec((1, HALF_ROPE, tm), lambda i: (i // nt, 0, i % nt)),
            pl.BlockSpec((1, HALF_ROPE, tm), lambda i: (i // nt, 0, i % nt)),
            _layer_resident((D_MODEL, W_IN_COLS), layer),
            _resident((1, Q_LORA_RANK)),
            _resident((1, KV_LORA_RANK)),
            _resident((N_HEADS * QK_HEAD_DIM, Q_LORA_RANK)),
            _resident((KV_LORA_RANK, N_HEADS * QK_NOPE_DIM)),
            _resident((N_HEADS * V_HEAD_DIM, KV_LORA_RANK)),
            _resident((FOURIER_GROUPS, FOURIER_GROUP_DIM, 2 * FOURIER_GROUP_DIM)),
        ],
        out_specs=(
            pl.BlockSpec((1, 2, FOURIER_GROUPS, tm, FOURIER_GROUP_DIM),
                         lambda i: (i // nt, 0, 0, i % nt, 0)),
            pl.BlockSpec((1, FOURIER_GROUPS, tm, FOURIER_GROUP_DIM),
                         lambda i: (i // nt, 0, i % nt, 0)),
            pl.BlockSpec((tm, D_ATTN), row),
            pl.BlockSpec((1, N_HEADS, QK_PAD_DIM, tm), bh),
            pl.BlockSpec((1, N_HEADS, tm, QK_PAD_DIM), lambda i: (i // nt, 0, i % nt, 0)),
            pl.BlockSpec((1, N_HEADS, V_HEAD_DIM, tm), bh),
        ),
        compiler_params=pltpu.CompilerParams(
            dimension_semantics=("arbitrary",), vmem_limit_bytes=VMEM_LIMIT),
        name="token_stage",
    )(x2d, scale, shift, cos_t, sin_t,
      w_in_all, w["q_norm"], w["kv_norm"], w["wq_t"], w["wk"], w["wv_t"], w["ab"])


DFT_LEVELS = 3
DFT_BLOCKS = 2 ** DFT_LEVELS
DFT_SUB = SEQ // DFT_BLOCKS
DFT_GROUPS_PER_STEP = 2
DFT_PASSES = ((0, 3),)


def _dft_constants():
    tw = []
    for level in range(DFT_LEVELS):
        length = SEQ >> level
        ang = 2.0 * np.pi * np.arange(length // 2) / length
        for f in (np.cos, np.sin):
            col = jnp.asarray(f(ang).astype(np.float32)).reshape(-1, 1)
            tw.append(jnp.broadcast_to(col, (length // 2, FOURIER_GROUP_DIM)))
    idx = np.arange(DFT_SUB)
    ang = 2.0 * np.pi * ((idx[:, None] * idx[None, :]) % DFT_SUB) / DFT_SUB
    mat = np.concatenate([np.cos(ang), -np.sin(ang)], axis=1) / math.sqrt(SEQ)
    return tw, jnp.asarray(mat.astype(np.float32)).astype(BF16)


def _bit_reverse(j, bits):
    return int(format(j, "0{}b".format(bits))[::-1], 2)


def _dft_kernel(*refs):
    tw_refs = refs[:2 * DFT_LEVELS]
    m_ref, pq_ref, g_ref, o_ref, p_scr, w_scr = refs[2 * DFT_LEVELS:2 * DFT_LEVELS + 6]
    y_scr = refs[2 * DFT_LEVELS + 6:]
    gps, c = DFT_GROUPS_PER_STEP, FOURIER_GROUP_DIM

    for first_level, n_levels in DFT_PASSES:
        from_input = first_level == 0
        rc = 16 if from_input else 8
        n_blocks = 2 ** n_levels
        blk = SEQ >> (first_level + n_levels)

        def body(i, carry, first_level=first_level, n_levels=n_levels, from_input=from_input,
                 rc=rc, n_blocks=n_blocks, blk=blk):
            r = pl.multiple_of(i * rc, rc)
            tw = {}
            for t in range(n_levels):
                for j in range(n_blocks >> (t + 1)):
                    rows = pl.ds(j * blk + r, rc)
                    tw[t, j] = (tw_refs[2 * (first_level + t)][rows, :],
                                tw_refs[2 * (first_level + t) + 1][rows, :])
            for sub in range(2 ** first_level):
                base = sub * n_blocks * blk
                for g in range(gps):
                    lanes = slice(g * c, (g + 1) * c)
                    data = []
                    for j in range(n_blocks):
                        rows = pl.ds(base + j * blk + r, rc)
                        if from_input:
                            data.append((pq_ref[0, 0, g, rows, :].astype(F32),
                                         pq_ref[0, 1, g, rows, :].astype(F32)))
                        else:
                            data.append((p_scr[rows, lanes], w_scr[rows, lanes]))
                    for t in range(n_levels):
                        half = n_blocks >> (t + 1)
                        for g0 in range(0, n_blocks, 2 * half):
                            for j in range(half):
                                (pa, wa), (pb, wb) = data[g0 + j], data[g0 + j + half]
                                cos, sin = tw[t, j]
                                dp, dw = pa - pb, wa - wb
                                data[g0 + j] = (pa + pb, wa + wb)
                                data[g0 + j + half] = (dp * cos - dw * sin, dw * cos + dp * sin)
                    for j in range(n_blocks):
                        rows = pl.ds(base + j * blk + r, rc)
                        p_scr[rows, lanes] = data[j][0]
                        w_scr[rows, lanes] = data[j][1]
            return carry

        lax.fori_loop(0, blk // rc, body, 0, unroll=2)

    for j in range(DFT_BLOCKS):
        rows = slice(j * DFT_SUB, (j + 1) * DFT_SUB)
        z = jnp.concatenate([p_scr[rows, :], w_scr[rows, :]], axis=0).astype(BF16)
        y = jnp.dot(m_ref[...], z, preferred_element_type=F32)
        out_rows = pl.ds(_bit_reverse(j, DFT_LEVELS), DFT_SUB, stride=DFT_BLOCKS)
        for g in range(gps):
            y_scr[g][out_rows, :] = y[:, g * c:(g + 1) * c]
    for g in range(gps):
        o_ref[0, g] = (y_scr[g][...] * g_ref[0, g].astype(F32)).astype(BF16)


def _seq_dft(pq, gate_f, consts):
    tw, mat = consts
    gps, c = DFT_GROUPS_PER_STEP, FOURIER_GROUP_DIM
    n_gsteps = FOURIER_GROUPS // gps
    return pl.pallas_call(
        _dft_kernel,
        out_shape=jax.ShapeDtypeStruct((BATCH, FOURIER_GROUPS, SEQ, c), BF16),
        grid=(BATCH, n_gsteps),
        in_specs=(
            [_resident(t.shape) for t in tw]
            + [_resident(mat.shape),
               pl.BlockSpec((1, 2, gps, SEQ, c), lambda b, gi: (b, 0, gi, 0, 0)),
               pl.BlockSpec((1, gps, SEQ, c), lambda b, gi: (b, gi, 0, 0))]),
        out_specs=pl.BlockSpec((1, gps, SEQ, c), lambda b, gi: (b, gi, 0, 0)),
        scratch_shapes=(
            [pltpu.VMEM((SEQ, gps * c), F32)] * 2 + [pltpu.VMEM((SEQ, c), F32)] * gps),
        compiler_params=pltpu.CompilerParams(
            dimension_semantics=("arbitrary", "arbitrary"), vmem_limit_bytes=VMEM_LIMIT),
        name="seq_dft",
    )(*tw, mat, pq, gate_f)


ATTN_PROBE_KEYS = 128
ATTN_MAX_DENOM = 2.0 ** 16


def _attn_kernel(q_t_ref, k_ref, v_t_ref, g_ref, o_ref, q_aug, s_scr, p_scr, *, tk):
    tq = q_t_ref.shape[3]
    n_chunks = SEQ // tk

    def finish(acc, l):
        o = (acc * (1.0 / l)).T
        o_ref[...] = (o * g_ref[...].astype(F32)).astype(BF16)

    s_probe = jnp.dot(k_ref[0, 0, 0:ATTN_PROBE_KEYS, :], q_t_ref[0, 0], preferred_element_type=F32)
    stab = jnp.max(s_probe, axis=0, keepdims=True)
    q_aug[...] = q_t_ref[0, 0]
    pad_rows = QK_PAD_DIM - QK_HEAD_DIM
    first = lax.broadcasted_iota(jnp.int32, (pad_rows, tq), 0) == 0
    q_aug[QK_HEAD_DIM:QK_PAD_DIM, :] = jnp.where(first, -stab, 0.0).astype(BF16)
    def scores(c):
        s_scr[c % 2] = jnp.dot(k_ref[0, 0, c * tk:(c + 1) * tk, :], q_aug[...],
                               preferred_element_type=F32)

    def probs(c):
        p = jnp.exp2(s_scr[c % 2])
        p_scr[c % 2] = p.astype(BF16)
        return jnp.sum(p, axis=0, keepdims=True)

    def values(c):
        return jnp.dot(v_t_ref[0, 0, :, c * tk:(c + 1) * tk], p_scr[c % 2],
                       preferred_element_type=F32)

    scores(0)
    scores(1)
    l = probs(0)
    acc = jnp.zeros((V_HEAD_DIM, tq), F32)
    for c in range(1, n_chunks):
        if c + 1 < n_chunks:
            scores(c + 1)
        l = l + probs(c)
        acc = acc + values(c - 1)
    acc = acc + values(n_chunks - 1)
    finish(acc, l)

    @pl.when(jnp.logical_not(jnp.max(l) <= ATTN_MAX_DENOM))
    def _():
        m = jnp.full((1, tq), -jnp.inf, F32)
        l2 = jnp.zeros((1, tq), F32)
        acc2 = jnp.zeros((V_HEAD_DIM, tq), F32)
        for c in range(n_chunks):
            s = jnp.dot(k_ref[0, 0, c * tk:(c + 1) * tk, :], q_t_ref[0, 0],
                        preferred_element_type=F32)
            m_new = jnp.maximum(m, jnp.max(s, axis=0, keepdims=True))
            alpha = jnp.exp2(m - m_new)
            p = jnp.exp2(s - m_new)
            l2 = alpha * l2 + jnp.sum(p, axis=0, keepdims=True)
            acc2 = alpha * acc2 + jnp.dot(v_t_ref[0, 0, :, c * tk:(c + 1) * tk], p.astype(BF16),
                                          preferred_element_type=F32)
            m = m_new
        finish(acc2, l2)


def _attention(q_t, k, v_t, gate_a):
    tq, tk = 2048, 512
    nq = SEQ // tq
    return pl.pallas_call(
        functools.partial(_attn_kernel, tk=tk),
        scratch_shapes=[
            pltpu.VMEM((QK_PAD_DIM, tq), BF16),
            pltpu.VMEM((2, tk, tq), F32),
            pltpu.VMEM((2, tk, tq), BF16),
        ],
        out_shape=jax.ShapeDtypeStruct((TOKENS, D_ATTN), BF16),
        grid=(BATCH, N_HEADS, nq),
        in_specs=[
            pl.BlockSpec((1, 1, QK_PAD_DIM, tq), lambda b, h, i: (b, h, 0, i)),
            pl.BlockSpec((1, 1, SEQ, QK_PAD_DIM), lambda b, h, i: (b, h, 0, 0)),
            pl.BlockSpec((1, 1, V_HEAD_DIM, SEQ), lambda b, h, i: (b, h, 0, 0)),
            pl.BlockSpec((tq, V_HEAD_DIM), lambda b, h, i: (b * nq + i, h)),
        ],
        out_specs=pl.BlockSpec((tq, V_HEAD_DIM), lambda b, h, i: (b * nq + i, h)),
        compiler_params=pltpu.CompilerParams(
            dimension_semantics=("arbitrary", "arbitrary", "arbitrary"),
            vmem_limit_bytes=VMEM_LIMIT),
        name="attention",
    )(q_t, k, v_t, gate_a)


OUT_SUB_ROWS = 256


def _out_kernel(yf_ref, ya_ref, x_ref, gate_ref, w_ref, g_ref, b_ref, o_ref):
    for rows in (slice(i * OUT_SUB_ROWS, (i + 1) * OUT_SUB_ROWS)
                 for i in range(x_ref.shape[0] // OUT_SUB_ROWS)):
        mixed = jnp.concatenate(
            [yf_ref[0, g, rows, :] for g in range(FOURIER_GROUPS)] + [ya_ref[rows, :]], axis=1)
        y = jnp.dot(mixed, w_ref[0], preferred_element_type=F32)
        r = DEEPNORM_ALPHA * x_ref[rows, :] + gate_ref[0] * y
        mu = jnp.mean(r, axis=-1, keepdims=True)
        rc = r - mu
        var = jnp.mean(rc * rc, axis=-1, keepdims=True)
        o_ref[rows, :] = rc * lax.rsqrt(var + NORM_EPS) * g_ref[...] + b_ref[...]


def _output_stage(y_f, y_a, x2d, gate, w_out_bf16, ln_g, ln_b, layer):
    tm = 1024
    nt = SEQ // tm
    row = lambda i: (i, 0)
    return pl.pallas_call(
        _out_kernel,
        out_shape=jax.ShapeDtypeStruct((TOKENS, D_MODEL), F32),
        grid=(TOKENS // tm,),
        in_specs=[
            pl.BlockSpec((1, FOURIER_GROUPS, tm, FOURIER_GROUP_DIM),
                         lambda i: (i // nt, 0, i % nt, 0)),
            pl.BlockSpec((tm, D_ATTN), row),
            pl.BlockSpec((tm, D_MODEL), row),
            pl.BlockSpec((1, 1, D_MODEL), lambda i: (i // nt, 0, 0)),
            _layer_resident((D_MODEL, D_MODEL), layer),
            _resident((1, D_MODEL)),
            _resident((1, D_MODEL)),
        ],
        out_specs=pl.BlockSpec((tm, D_MODEL), row),
        compiler_params=pltpu.CompilerParams(
            dimension_semantics=("arbitrary",), vmem_limit_bytes=VMEM_LIMIT),
        name="output_stage",
    )(y_f, y_a, x2d, gate, w_out_bf16, ln_g, ln_b)


def _w_in_kernel(wt_ref, o_ref):
    pad = W_IN_COLS - wt_ref.shape[1]
    split = COL_ZA - pad
    aligned = COL_ZA - 128
    o_ref[0, :, 0:aligned] = wt_ref[0, 0:aligned, :].T.astype(BF16)
    mid = wt_ref[0, aligned:aligned + 128, :].T
    lane = lax.broadcasted_iota(jnp.int32, mid.shape, 1)
    o_ref[0, :, aligned:COL_ZA] = jnp.where(lane < split - aligned, mid, 0.0).astype(BF16)
    o_ref[0, :, COL_ZA:W_IN_COLS] = wt_ref[0, split:, :].T.astype(BF16)


def _prepare_w_in(w_in):
    rows = 512
    cols = w_in.shape[2]
    return pl.pallas_call(
        _w_in_kernel,
        out_shape=jax.ShapeDtypeStruct((DEPTH, D_MODEL, W_IN_COLS), BF16),
        grid=(DEPTH, D_MODEL // rows),
        in_specs=[pl.BlockSpec((1, cols, rows), lambda l, i: (l, 0, i))],
        out_specs=pl.BlockSpec((1, rows, W_IN_COLS), lambda l, i: (l, i, 0)),
        compiler_params=pltpu.CompilerParams(
            dimension_semantics=("arbitrary", "arbitrary"), vmem_limit_bytes=VMEM_LIMIT),
        name="prepare_w_in",
    )(jnp.swapaxes(w_in, 1, 2))


def _layer_weights(q_norm_l, w_q_b_l, kv_norm_l, w_kv_b_l, w_fmix_l):
    wq = w_q_b_l.reshape(Q_LORA_RANK, N_HEADS, QK_HEAD_DIM)
    wq = jnp.concatenate([
        wq[:, :, :QK_NOPE_DIM].reshape(Q_LORA_RANK, -1),
        wq[:, :, QK_NOPE_DIM:QK_NOPE_DIM + HALF_ROPE].reshape(Q_LORA_RANK, -1),
        wq[:, :, QK_NOPE_DIM + HALF_ROPE:].reshape(Q_LORA_RANK, -1)], axis=1)
    wkv = w_kv_b_l.reshape(KV_LORA_RANK, N_HEADS, QK_NOPE_DIM + V_HEAD_DIM)
    wk = wkv[:, :, :QK_NOPE_DIM].reshape(KV_LORA_RANK, -1)
    wv = wkv[:, :, QK_NOPE_DIM:].reshape(KV_LORA_RANK, -1)
    return {
        "q_norm": q_norm_l.reshape(1, Q_LORA_RANK),
        "kv_norm": kv_norm_l.reshape(1, KV_LORA_RANK),
        "wq_t": wq.T.astype(BF16),
        "wk": wk.astype(BF16),
        "wv_t": wv.T.astype(BF16),
        "ab": _fold_fourier_weights(w_fmix_l),
    }


def kernel(x, c, positions, w_ada, b_ada, w_in, q_norm, w_q_b, kv_norm, w_kv_b, w_fmix, w_out,
           ln_g, ln_b):
    assert x.shape == (BATCH, SEQ, D_MODEL) and w_ada.shape[0] == DEPTH
    mod = _ada_modulation(c, w_ada, b_ada)
    tables = _rope_tables(positions)
    dft_consts = _dft_constants()
    w_in = _prepare_w_in(w_in)
    w_out_bf16 = w_out.astype(BF16)
    x2d = x.reshape(TOKENS, D_MODEL)
    for l in range(DEPTH):
        shift, scale, gate = (
            mod[l, :BATCH, i * D_MODEL:(i + 1) * D_MODEL].reshape(BATCH, 1, D_MODEL)
            for i in range(3))
        w = _layer_weights(q_norm[l], w_q_b[l], kv_norm[l], w_kv_b[l], w_fmix[l])
        pq, gate_f, gate_a, q_t, k, v_t = _token_stage(x2d, scale, shift, tables, w, w_in, l)
        y_f = _seq_dft(pq, gate_f, dft_consts)
        y_a = _attention(q_t, k, v_t, gate_a)
        x2d = _output_stage(y_f, y_a, x2d, gate, w_out_bf16,
                            ln_g[l].reshape(1, D_MODEL), ln_b[l].reshape(1, D_MODEL), l)
    return x2d.reshape(BATCH, SEQ, D_MODEL)
```

```python
import functools
import math

import numpy as np
import jax
import jax.numpy as jnp
from jax import lax
from jax.experimental import pallas as pl
from jax.experimental.pallas import tpu as pltpu

D_MODEL = 2048
BATCH = 4
SEQ = 4096
DEPTH = 2
D_FOURIER = D_MODEL // 2
FOURIER_GROUPS = 8
FOURIER_GROUP_DIM = D_FOURIER // FOURIER_GROUPS
V_HEAD_DIM = 128
D_ATTN = D_MODEL // 2
N_HEADS = D_ATTN // V_HEAD_DIM
QK_NOPE_DIM = 128
QK_ROPE_DIM = 64
QK_HEAD_DIM = QK_NOPE_DIM + QK_ROPE_DIM
Q_LORA_RANK = D_MODEL // 4
KV_LORA_RANK = D_MODEL // 8
ROPE_THETA = 10000.0
NORM_EPS = 1e-6
DEEPNORM_ALPHA = (2 * DEPTH) ** 0.25

TOKENS = BATCH * SEQ
HALF_ROPE = QK_ROPE_DIM // 2
QK_PAD_DIM = 256
W_IN_COLS = 3968
COL_U, COL_ZF, COL_CQ, COL_CKV, COL_ZA = 0, 1024, 2048, 2560, 2944

F32 = jnp.float32
BF16 = jnp.bfloat16
VMEM_LIMIT = 52 * 1024 * 1024

Q_SCALE = math.log2(math.e) / math.sqrt(QK_HEAD_DIM)


def _silu(v):
    return v * (1.0 / (1.0 + jnp.exp(-v)))


def _nt_dot(a, b):
    return lax.dot_general(a, b, (((1,), (1,)), ((), ())), preferred_element_type=F32)


def _ada_kernel(c_ref, w_ref, b_ref, o_ref):
    c_act = _silu(c_ref[...]).astype(BF16)
    w = w_ref[0].astype(BF16)
    o_ref[0] = jnp.dot(c_act, w, preferred_element_type=F32) + b_ref[0]


def _ada_modulation(c, w_ada, b_ada):
    tn = 768
    c_pad = jnp.pad(c, ((0, 8 - BATCH), (0, 0)))
    b3 = b_ada.reshape(DEPTH, 1, 3 * D_MODEL)
    return pl.pallas_call(
        _ada_kernel,
        out_shape=jax.ShapeDtypeStruct((DEPTH, 8, 3 * D_MODEL), F32),
        grid=(DEPTH, 3 * D_MODEL // tn),
        in_specs=[
            pl.BlockSpec((8, D_MODEL), lambda l, j: (0, 0)),
            pl.BlockSpec((1, D_MODEL, tn), lambda l, j: (l, 0, j)),
            pl.BlockSpec((1, 1, tn), lambda l, j: (l, 0, j)),
        ],
        out_specs=pl.BlockSpec((1, 8, tn), lambda l, j: (l, 0, j)),
        compiler_params=pltpu.CompilerParams(
            dimension_semantics=("arbitrary", "arbitrary"), vmem_limit_bytes=VMEM_LIMIT),
        name="ada_modulation",
    )(c_pad, w_ada, b3)


def _rope_kernel(pos_ref, invf_ref, cos_t_ref, sin_t_ref):
    ang_t = invf_ref[...] * pos_ref[0].astype(F32)
    cos_t_ref[0] = jnp.cos(ang_t)
    sin_t_ref[0] = jnp.sin(ang_t)


def _rope_tables(positions):
    ts = 2048
    inv_freq = ROPE_THETA ** (-jnp.arange(0, QK_ROPE_DIM, 2, dtype=F32) / QK_ROPE_DIM)
    table = jax.ShapeDtypeStruct((BATCH, HALF_ROPE, SEQ), F32)
    return pl.pallas_call(
        _rope_kernel,
        out_shape=(table, table),
        grid=(BATCH, SEQ // ts),
        in_specs=[
            pl.BlockSpec((1, 1, ts), lambda b, s: (b, 0, s)),
            pl.BlockSpec((HALF_ROPE, 1), lambda b, s: (0, 0)),
        ],
        out_specs=(
            pl.BlockSpec((1, HALF_ROPE, ts), lambda b, s: (b, 0, s)),
            pl.BlockSpec((1, HALF_ROPE, ts), lambda b, s: (b, 0, s)),
        ),
        compiler_params=pltpu.CompilerParams(
            dimension_semantics=("arbitrary", "arbitrary"), vmem_limit_bytes=VMEM_LIMIT),
        name="rope_tables",
    )(positions.reshape(BATCH, 1, SEQ), inv_freq.reshape(HALF_ROPE, 1))


def _channel_dft_matrix():
    idx = np.arange(FOURIER_GROUP_DIM)
    ang = 2.0 * np.pi * ((idx[:, None] * idx[None, :]) % FOURIER_GROUP_DIM) / FOURIER_GROUP_DIM
    scale = 1.0 / math.sqrt(FOURIER_GROUP_DIM)
    return np.concatenate([np.cos(ang), np.sin(ang)], axis=1).astype(np.float32) * np.float32(scale)


def _fold_kernel(cs_ref, w_ref, o_ref):
    w = w_ref[0]
    dims = (((1,), (0,)), ((), ()))
    a = lax.dot_general(cs_ref[:, :FOURIER_GROUP_DIM], w, dims,
                        precision=lax.Precision.HIGHEST, preferred_element_type=F32)
    b = lax.dot_general(cs_ref[:, FOURIER_GROUP_DIM:], w, dims,
                        precision=lax.Precision.HIGHEST, preferred_element_type=F32)
    o_ref[0] = jnp.concatenate([a, b], axis=1).astype(BF16)


def _fold_fourier_weights(w_fmix_l):
    g, c = FOURIER_GROUPS, FOURIER_GROUP_DIM
    return pl.pallas_call(
        _fold_kernel,
        out_shape=jax.ShapeDtypeStruct((g, c, 2 * c), BF16),
        grid=(g,),
        in_specs=[
            pl.BlockSpec((c, 2 * c), lambda i: (0, 0)),
            pl.BlockSpec((1, c, c), lambda i: (i, 0, 0)),
        ],
        out_specs=pl.BlockSpec((1, c, 2 * c), lambda i: (i, 0, 0)),
        compiler_params=pltpu.CompilerParams(dimension_semantics=("arbitrary",)),
        name="fold_fourier_weights",
    )(jnp.asarray(_channel_dft_matrix()), w_fmix_l)


def _token_kernel(x_ref, scale_ref, shift_ref, cos_t_ref, sin_t_ref,
                  w_in_ref, qn_ref, kvn_ref, wq_t_ref, wk_ref, wv_t_ref, ab_ref,
                  pq_ref, gf_ref, ga_ref, q_t_ref, k_ref, v_t_ref):
    x = x_ref[...]
    tm = x.shape[0]
    mu = jnp.mean(x, axis=-1, keepdims=True)
    xc = x - mu
    var = jnp.mean(xc * xc, axis=-1, keepdims=True)
    h = xc * lax.rsqrt(var + NORM_EPS) * (1.0 + scale_ref[0]) + shift_ref[0]
    hb = h.astype(BF16)

    def proj(lo, hi):
        return jnp.dot(hb, w_in_ref[0, :, lo:hi], preferred_element_type=F32)

    p_a = proj(COL_U, COL_CQ)
    p_b = proj(COL_CQ, W_IN_COLS)
    ub = p_a[:, :COL_ZF].astype(BF16)
    c = FOURIER_GROUP_DIM
    gate_f = _silu(p_a[:, COL_ZF:]).astype(BF16)
    for g in range(FOURIER_GROUPS):
        pq = jnp.dot(ub[:, g * c:(g + 1) * c], ab_ref[g], preferred_element_type=F32)
        pq_ref[0, 0, g] = pq[:, :c].astype(BF16)
        pq_ref[0, 1, g] = pq[:, c:].astype(BF16)
        gf_ref[0, g] = gate_f[:, g * c:(g + 1) * c]

    gate_a = _silu(p_b[:, COL_ZA - COL_CQ:]).astype(BF16)
    for hd in range(N_HEADS):
        ga_ref[0, hd] = gate_a[:, hd * V_HEAD_DIM:(hd + 1) * V_HEAD_DIM]

    cq = p_b[:, :COL_CKV - COL_CQ]
    cqn = (cq * lax.rsqrt(jnp.mean(cq * cq, axis=-1, keepdims=True) + NORM_EPS)
           * qn_ref[...]).astype(BF16)
    q_t = _nt_dot(wq_t_ref[...], cqn) * Q_SCALE
    cos_t = cos_t_ref[0]
    sin_t = sin_t_ref[0]
    n_nope = N_HEADS * QK_NOPE_DIM
    n_half = N_HEADS * HALF_ROPE
    zeros_q = jnp.zeros((QK_PAD_DIM - QK_HEAD_DIM, tm), BF16)
    for hd in range(N_HEADS):
        q_t_ref[0, hd, 0:QK_NOPE_DIM, :] = (
            q_t[hd * QK_NOPE_DIM:(hd + 1) * QK_NOPE_DIM].astype(BF16))
        x1 = q_t[n_nope + hd * HALF_ROPE:n_nope + (hd + 1) * HALF_ROPE]
        x2 = q_t[n_nope + n_half + hd * HALF_ROPE:n_nope + n_half + (hd + 1) * HALF_ROPE]
        q_t_ref[0, hd, QK_NOPE_DIM:QK_NOPE_DIM + HALF_ROPE, :] = (
            x1 * cos_t - x2 * sin_t).astype(BF16)
        q_t_ref[0, hd, QK_NOPE_DIM + HALF_ROPE:QK_HEAD_DIM, :] = (
            x2 * cos_t + x1 * sin_t).astype(BF16)
        q_t_ref[0, hd, QK_HEAD_DIM:QK_PAD_DIM, :] = zeros_q

    ckv_kr = p_b[:, COL_CKV - COL_CQ:COL_ZA - COL_CQ]
    ckv = ckv_kr[:, :KV_LORA_RANK]
    ckvn = (ckv * lax.rsqrt(jnp.mean(ckv * ckv, axis=-1, keepdims=True) + NORM_EPS)
            * kvn_ref[...]).astype(BF16)
    k_nope = jnp.dot(ckvn, wk_ref[...], preferred_element_type=F32)
    v_t = _nt_dot(wv_t_ref[...], ckvn)
    kr_t = ckv_kr[:, KV_LORA_RANK:].T
    k1, k2 = kr_t[:HALF_ROPE], kr_t[HALF_ROPE:QK_ROPE_DIM]
    pad_row = lax.broadcasted_iota(jnp.int32, (QK_PAD_DIM - QK_HEAD_DIM, tm), 0)
    kr_pad = jnp.concatenate(
        [k1 * cos_t - k2 * sin_t, k2 * cos_t + k1 * sin_t, jnp.where(pad_row == 0, 1.0, 0.0)],
        axis=0).T.astype(BF16)
    for hd in range(N_HEADS):
        k_ref[0, hd, :, 0:QK_NOPE_DIM] = (
            k_nope[:, hd * QK_NOPE_DIM:(hd + 1) * QK_NOPE_DIM].astype(BF16))
        k_ref[0, hd, :, QK_NOPE_DIM:QK_PAD_DIM] = kr_pad
        v_t_ref[0, hd] = v_t[hd * V_HEAD_DIM:(hd + 1) * V_HEAD_DIM].astype(BF16)


def _resident(shape):
    zeros = (0,) * len(shape)
    return pl.BlockSpec(shape, lambda *_: zeros, pipeline_mode=pl.Buffered(1))


def _layer_resident(shape, layer):
    index = (layer,) + (0,) * len(shape)
    return pl.BlockSpec((1,) + tuple(shape), lambda *_: index, pipeline_mode=pl.Buffered(1))


def _token_stage(x2d, scale, shift, tables, w, w_in_all, layer):
    tm = 512
    nt = SEQ // tm
    cos_t, sin_t = tables
    row = lambda i: (i, 0)
    per_batch = lambda i: (i // nt, 0, 0)
    bh = lambda i: (i // nt, 0, 0, i % nt)
    return pl.pallas_call(
        _token_kernel,
        out_shape=(
            jax.ShapeDtypeStruct((BATCH, 2, FOURIER_GROUPS, SEQ, FOURIER_GROUP_DIM), BF16),
            jax.ShapeDtypeStruct((BATCH, FOURIER_GROUPS, SEQ, FOURIER_GROUP_DIM), BF16),
            jax.ShapeDtypeStruct((BATCH, N_HEADS, SEQ, V_HEAD_DIM), BF16),
            jax.ShapeDtypeStruct((BATCH, N_HEADS, QK_PAD_DIM, SEQ), BF16),
            jax.ShapeDtypeStruct((BATCH, N_HEADS, SEQ, QK_PAD_DIM), BF16),
            jax.ShapeDtypeStruct((BATCH, N_HEADS, V_HEAD_DIM, SEQ), BF16),
        ),
        grid=(TOKENS // tm,),
        in_specs=[
            pl.BlockSpec((tm, D_MODEL), row),
            pl.BlockSpec((1, 1, D_MODEL), per_batch),
            pl.BlockSpec((1, 1, D_MODEL), per_batch),
            pl.BlockSpec((1, HALF_ROPE, tm), lambda i: (i // nt, 0, i % nt)),
            pl.BlockSpec((1, HALF_ROPE, tm), lambda i: (i // nt, 0, i % nt)),
            _layer_resident((D_MODEL, W_IN_COLS), layer),
            _resident((1, Q_LORA_RANK)),
            _resident((1, KV_LORA_RANK)),
            _resident((N_HEADS * QK_HEAD_DIM, Q_LORA_RANK)),
            _resident((KV_LORA_RANK, N_HEADS * QK_NOPE_DIM)),
            _resident((N_HEADS * V_HEAD_DIM, KV_LORA_RANK)),
            _resident((FOURIER_GROUPS, FOURIER_GROUP_DIM, 2 * FOURIER_GROUP_DIM)),
        ],
        out_specs=(
            pl.BlockSpec((1, 2, FOURIER_GROUPS, tm, FOURIER_GROUP_DIM),
                         lambda i: (i // nt, 0, 0, i % nt, 0)),
            pl.BlockSpec((1, FOURIER_GROUPS, tm, FOURIER_GROUP_DIM),
                         lambda i: (i // nt, 0, i % nt, 0)),
            pl.BlockSpec((1, N_HEADS, tm, V_HEAD_DIM), lambda i: (i // nt, 0, i % nt, 0)),
            pl.BlockSpec((1, N_HEADS, QK_PAD_DIM, tm), bh),
            pl.BlockSpec((1, N_HEADS, tm, QK_PAD_DIM), lambda i: (i // nt, 0, i % nt, 0)),
            pl.BlockSpec((1, N_HEADS, V_HEAD_DIM, tm), bh),
        ),
        compiler_params=pltpu.CompilerParams(
            dimension_semantics=("arbitrary",), vmem_limit_bytes=VMEM_LIMIT),
        name="token_stage",
    )(x2d, scale, shift, cos_t, sin_t,
      w_in_all, w["q_norm"], w["kv_norm"], w["wq_t"], w["wk"], w["wv_t"], w["ab"])


DFT_LEVELS = 3
DFT_BLOCKS = 2 ** DFT_LEVELS
DFT_SUB = SEQ // DFT_BLOCKS
DFT_GROUPS_PER_STEP = 2
DFT_PASSES = ((0, 3),)


def _dft_constants():
    tw = []
    for level in range(DFT_LEVELS):
        length = SEQ >> level
        ang = 2.0 * np.pi * np.arange(length // 2) / length
        for f in (np.cos, np.sin):
            col = jnp.asarray(f(ang).astype(np.float32)).reshape(-1, 1)
            tw.append(jnp.broadcast_to(col, (length // 2, FOURIER_GROUP_DIM)))
    idx = np.arange(DFT_SUB)
    ang = 2.0 * np.pi * ((idx[:, None] * idx[None, :]) % DFT_SUB) / DFT_SUB
    mat = np.concatenate([np.cos(ang), -np.sin(ang)], axis=1) / math.sqrt(SEQ)
    return tw, jnp.asarray(mat.astype(np.float32)).astype(BF16)


def _bit_reverse(j, bits):
    return int(format(j, "0{}b".format(bits))[::-1], 2)


def _dft_kernel(*refs):
    tw_refs = refs[:2 * DFT_LEVELS]
    m_ref, pq_ref, g_ref, o_ref, p_scr, w_scr = refs[2 * DFT_LEVELS:2 * DFT_LEVELS + 6]
    y_scr = refs[2 * DFT_LEVELS + 6:]
    gps, c = DFT_GROUPS_PER_STEP, FOURIER_GROUP_DIM

    for first_level, n_levels in DFT_PASSES:
        from_input = first_level == 0
        rc = 16 if from_input else 8
        n_blocks = 2 ** n_levels
        blk = SEQ >> (first_level + n_levels)

        def body(i, carry, first_level=first_level, n_levels=n_levels, from_input=from_input,
                 rc=rc, n_blocks=n_blocks, blk=blk):
            r = pl.multiple_of(i * rc, rc)
            tw = {}
            for t in range(n_levels):
                for j in range(n_blocks >> (t + 1)):
                    rows = pl.ds(j * blk + r, rc)
                    tw[t, j] = (tw_refs[2 * (first_level + t)][rows, :],
                                tw_refs[2 * (first_level + t) + 1][rows, :])
            for sub in range(2 ** first_level):
                base = sub * n_blocks * blk
                for g in range(gps):
                    lanes = slice(g * c, (g + 1) * c)
                    data = []
                    for j in range(n_blocks):
                        rows = pl.ds(base + j * blk + r, rc)
                        if from_input:
                            data.append((pq_ref[0, 0, g, rows, :].astype(F32),
                                         pq_ref[0, 1, g, rows, :].astype(F32)))
                        else:
                            data.append((p_scr[rows, lanes], w_scr[rows, lanes]))
                    for t in range(n_levels):
                        half = n_blocks >> (t + 1)
                        for g0 in range(0, n_blocks, 2 * half):
                            for j in range(half):
                                (pa, wa), (pb, wb) = data[g0 + j], data[g0 + j + half]
                                cos, sin = tw[t, j]
                                dp, dw = pa - pb, wa - wb
                                data[g0 + j] = (pa + pb, wa + wb)
                                data[g0 + j + half] = (dp * cos - dw * sin, dw * cos + dp * sin)
                    for j in range(n_blocks):
                        rows = pl.ds(base + j * blk + r, rc)
                        p_scr[rows, lanes] = data[j][0]
                        w_scr[rows, lanes] = data[j][1]
            return carry

        lax.fori_loop(0, blk // rc, body, 0, unroll=2)

    for j in range(DFT_BLOCKS):
        rows = slice(j * DFT_SUB, (j + 1) * DFT_SUB)
        z = jnp.concatenate([p_scr[rows, :], w_scr[rows, :]], axis=0).astype(BF16)
        y = jnp.dot(m_ref[...], z, preferred_element_type=F32)
        out_rows = pl.ds(_bit_reverse(j, DFT_LEVELS), DFT_SUB, stride=DFT_BLOCKS)
        for g in range(gps):
            y_scr[g][out_rows, :] = y[:, g * c:(g + 1) * c]
    for g in range(gps):
        o_ref[0, g] = (y_scr[g][...] * g_ref[0, g].astype(F32)).astype(BF16)


def _seq_dft(pq, gate_f, consts):
    tw, mat = consts
    gps, c = DFT_GROUPS_PER_STEP, FOURIER_GROUP_DIM
    n_gsteps = FOURIER_GROUPS // gps
    return pl.pallas_call(
        _dft_kernel,
        out_shape=jax.ShapeDtypeStruct((BATCH, FOURIER_GROUPS, SEQ, c), BF16),
        grid=(BATCH, n_gsteps),
        in_specs=(
            [_resident(t.shape) for t in tw]
            + [_resident(mat.shape),
               pl.BlockSpec((1, 2, gps, SEQ, c), lambda b, gi: (b, 0, gi, 0, 0)),
               pl.BlockSpec((1, gps, SEQ, c), lambda b, gi: (b, gi, 0, 0))]),
        out_specs=pl.BlockSpec((1, gps, SEQ, c), lambda b, gi: (b, gi, 0, 0)),
        scratch_shapes=(
            [pltpu.VMEM((SEQ, gps * c), F32)] * 2 + [pltpu.VMEM((SEQ, c), F32)] * gps),
        compiler_params=pltpu.CompilerParams(
            dimension_semantics=("arbitrary", "arbitrary"), vmem_limit_bytes=VMEM_LIMIT),
        name="seq_dft",
    )(*tw, mat, pq, gate_f)


ATTN_PROBE_KEYS = 128
ATTN_MAX_DENOM = 2.0 ** 16


def _attn_kernel(q_t_ref, k_ref, v_t_ref, g_ref, o_ref, q_aug, s_scr, p_scr, *, tk):
    tq = q_t_ref.shape[3]
    n_chunks = SEQ // tk

    def finish(acc, l):
        o = (acc * (1.0 / l)).T
        o_ref[0, 0] = (o * g_ref[0, 0].astype(F32)).astype(BF16)

    s_probe = jnp.dot(k_ref[0, 0, 0:ATTN_PROBE_KEYS, :], q_t_ref[0, 0], preferred_element_type=F32)
    stab = jnp.max(s_probe, axis=0, keepdims=True)
    q_aug[...] = q_t_ref[0, 0]
    pad_rows = QK_PAD_DIM - QK_HEAD_DIM
    first = lax.broadcasted_iota(jnp.int32, (pad_rows, tq), 0) == 0
    q_aug[QK_HEAD_DIM:QK_PAD_DIM, :] = jnp.where(first, -stab, 0.0).astype(BF16)
    def scores(c):
        s_scr[c % 2] = jnp.dot(k_ref[0, 0, c * tk:(c + 1) * tk, :], q_aug[...],
                               preferred_element_type=F32)

    def probs(c):
        p = jnp.exp2(s_scr[c % 2])
        p_scr[c % 2] = p.astype(BF16)
        return jnp.sum(p, axis=0, keepdims=True)

    def values(c):
        return jnp.dot(v_t_ref[0, 0, :, c * tk:(c + 1) * tk], p_scr[c % 2],
                       preferred_element_type=F32)

    scores(0)
    scores(1)
    l = probs(0)
    acc = jnp.zeros((V_HEAD_DIM, tq), F32)
    for c in range(1, n_chunks):
        if c + 1 < n_chunks:
            scores(c + 1)
        l = l + probs(c)
        acc = acc + values(c - 1)
    acc = acc + values(n_chunks - 1)
    finish(acc, l)

    @pl.when(jnp.logical_not(jnp.max(l) <= ATTN_MAX_DENOM))
    def _():
        m = jnp.full((1, tq), -jnp.inf, F32)
        l2 = jnp.zeros((1, tq), F32)
        acc2 = jnp.zeros((V_HEAD_DIM, tq), F32)
        for c in range(n_chunks):
            s = jnp.dot(k_ref[0, 0, c * tk:(c + 1) * tk, :], q_t_ref[0, 0],
                        preferred_element_type=F32)
            m_new = jnp.maximum(m, jnp.max(s, axis=0, keepdims=True))
            alpha = jnp.exp2(m - m_new)
            p = jnp.exp2(s - m_new)
            l2 = alpha * l2 + jnp.sum(p, axis=0, keepdims=True)
            acc2 = alpha * acc2 + jnp.dot(v_t_ref[0, 0, :, c * tk:(c + 1) * tk], p.astype(BF16),
                                          preferred_element_type=F32)
            m = m_new
        finish(acc2, l2)


def _attention(q_t, k, v_t, gate_a):
    tq, tk = 2048, 512
    nq = SEQ // tq
    return pl.pallas_call(
        functools.partial(_attn_kernel, tk=tk),
        scratch_shapes=[
            pltpu.VMEM((QK_PAD_DIM, tq), BF16),
            pltpu.VMEM((2, tk, tq), F32),
            pltpu.VMEM((2, tk, tq), BF16),
        ],
        out_shape=jax.ShapeDtypeStruct((BATCH, N_HEADS, SEQ, V_HEAD_DIM), BF16),
        grid=(BATCH, N_HEADS, nq),
        in_specs=[
            pl.BlockSpec((1, 1, QK_PAD_DIM, tq), lambda b, h, i: (b, h, 0, i)),
            pl.BlockSpec((1, 1, SEQ, QK_PAD_DIM), lambda b, h, i: (b, h, 0, 0)),
            pl.BlockSpec((1, 1, V_HEAD_DIM, SEQ), lambda b, h, i: (b, h, 0, 0)),
            pl.BlockSpec((1, 1, tq, V_HEAD_DIM), lambda b, h, i: (b, h, i, 0)),
        ],
        out_specs=pl.BlockSpec((1, 1, tq, V_HEAD_DIM), lambda b, h, i: (b, h, i, 0)),
        compiler_params=pltpu.CompilerParams(
            dimension_semantics=("arbitrary", "arbitrary", "arbitrary"),
            vmem_limit_bytes=VMEM_LIMIT),
        name="attention",
    )(q_t, k, v_t, gate_a)


OUT_SUB_ROWS = 256


def _out_kernel(yf_ref, ya_ref, x_ref, gate_ref, w_ref, g_ref, b_ref, o_ref):
    for rows in (slice(i * OUT_SUB_ROWS, (i + 1) * OUT_SUB_ROWS)
                 for i in range(x_ref.shape[0] // OUT_SUB_ROWS)):
        mixed = jnp.concatenate(
            [yf_ref[0, g, rows, :] for g in range(FOURIER_GROUPS)]
            + [ya_ref[0, hd, rows, :] for hd in range(N_HEADS)], axis=1)
        y = jnp.dot(mixed, w_ref[0], preferred_element_type=F32)
        r = DEEPNORM_ALPHA * x_ref[rows, :] + gate_ref[0] * y
        mu = jnp.mean(r, axis=-1, keepdims=True)
        rc = r - mu
        var = jnp.mean(rc * rc, axis=-1, keepdims=True)
        o_ref[rows, :] = rc * lax.rsqrt(var + NORM_EPS) * g_ref[...] + b_ref[...]


def _output_stage(y_f, y_a, x2d, gate, w_out_bf16, ln_g, ln_b, layer):
    tm = 1024
    nt = SEQ // tm
    row = lambda i: (i, 0)
    return pl.pallas_call(
        _out_kernel,
        out_shape=jax.ShapeDtypeStruct((TOKENS, D_MODEL), F32),
        grid=(TOKENS // tm,),
        in_specs=[
            pl.BlockSpec((1, FOURIER_GROUPS, tm, FOURIER_GROUP_DIM),
                         lambda i: (i // nt, 0, i % nt, 0)),
            pl.BlockSpec((1, N_HEADS, tm, V_HEAD_DIM), lambda i: (i // nt, 0, i % nt, 0)),
            pl.BlockSpec((tm, D_MODEL), row),
            pl.BlockSpec((1, 1, D_MODEL), lambda i: (i // nt, 0, 0)),
            _layer_resident((D_MODEL, D_MODEL), layer),
            _resident((1, D_MODEL)),
            _resident((1, D_MODEL)),
        ],
        out_specs=pl.BlockSpec((tm, D_MODEL), row),
        compiler_params=pltpu.CompilerParams(
            dimension_semantics=("arbitrary",), vmem_limit_bytes=VMEM_LIMIT),
        name="output_stage",
    )(y_f, y_a, x2d, gate, w_out_bf16, ln_g, ln_b)


def _w_in_kernel(wt_ref, o_ref):
    pad = W_IN_COLS - wt_ref.shape[1]
    split = COL_ZA - pad
    aligned = COL_ZA - 128
    o_ref[0, :, 0:aligned] = wt_ref[0, 0:aligned, :].T.astype(BF16)
    mid = wt_ref[0, aligned:aligned + 128, :].T
    lane = lax.broadcasted_iota(jnp.int32, mid.shape, 1)
    o_ref[0, :, aligned:COL_ZA] = jnp.where(lane < split - aligned, mid, 0.0).astype(BF16)
    o_ref[0, :, COL_ZA:W_IN_COLS] = wt_ref[0, split:, :].T.astype(BF16)


def _prepare_w_in(w_in):
    rows = 512
    cols = w_in.shape[2]
    return pl.pallas_call(
        _w_in_kernel,
        out_shape=jax.ShapeDtypeStruct((DEPTH, D_MODEL, W_IN_COLS), BF16),
        grid=(DEPTH, D_MODEL // rows),
        in_specs=[pl.BlockSpec((1, cols, rows), lambda l, i: (l, 0, i))],
        out_specs=pl.BlockSpec((1, rows, W_IN_COLS), lambda l, i: (l, i, 0)),
        compiler_params=pltpu.CompilerParams(
            dimension_semantics=("arbitrary", "arbitrary"), vmem_limit_bytes=VMEM_LIMIT),
        name="prepare_w_in",
    )(jnp.swapaxes(w_in, 1, 2))


def _layer_weights(q_norm_l, w_q_b_l, kv_norm_l, w_kv_b_l, w_fmix_l):
    wq = w_q_b_l.reshape(Q_LORA_RANK, N_HEADS, QK_HEAD_DIM)
    wq = jnp.concatenate([
        wq[:, :, :QK_NOPE_DIM].reshape(Q_LORA_RANK, -1),
        wq[:, :, QK_NOPE_DIM:QK_NOPE_DIM + HALF_ROPE].reshape(Q_LORA_RANK, -1),
        wq[:, :, QK_NOPE_DIM + HALF_ROPE:].reshape(Q_LORA_RANK, -1)], axis=1)
    wkv = w_kv_b_l.reshape(KV_LORA_RANK, N_HEADS, QK_NOPE_DIM + V_HEAD_DIM)
    wk = wkv[:, :, :QK_NOPE_DIM].reshape(KV_LORA_RANK, -1)
    wv = wkv[:, :, QK_NOPE_DIM:].reshape(KV_LORA_RANK, -1)
    return {
        "q_norm": q_norm_l.reshape(1, Q_LORA_RANK),
        "kv_norm": kv_norm_l.reshape(1, KV_LORA_RANK),
        "wq_t": wq.T.astype(BF16),
        "wk": wk.astype(BF16),
        "wv_t": wv.T.astype(BF16),
        "ab": _fold_fourier_weights(w_fmix_l),
    }


def kernel(x, c, positions, w_ada, b_ada, w_in, q_norm, w_q_b, kv_norm, w_kv_b, w_fmix, w_out,
           ln_g, ln_b):
    assert x.shape == (BATCH, SEQ, D_MODEL) and w_ada.shape[0] == DEPTH
    mod = _ada_modulation(c, w_ada, b_ada)
    tables = _rope_tables(positions)
    dft_consts = _dft_constants()
    w_in = _prepare_w_in(w_in)
    w_out_bf16 = w_out.astype(BF16)
    x2d = x.reshape(TOKENS, D_MODEL)
    for l in range(DEPTH):
        shift, scale, gate = (
            mod[l, :BATCH, i * D_MODEL:(i + 1) * D_MODEL].reshape(BATCH, 1, D_MODEL)
            for i in range(3))
        w = _layer_weights(q_norm[l], w_q_b[l], kv_norm[l], w_kv_b[l], w_fmix[l])
        pq, gate_f, gate_a, q_t, k, v_t = _token_stage(x2d, scale, shift, tables, w, w_in, l)
        y_f = _seq_dft(pq, gate_f, dft_consts)
        y_a = _attention(q_t, k, v_t, gate_a)
        x2d = _output_stage(y_f, y_a, x2d, gate, w_out_bf16,
                            ln_g[l].reshape(1, D_MODEL), ln_b[l].reshape(1, D_MODEL), l)
    return x2d.reshape(BATCH, SEQ, D_MODEL)
```
